```python
import math
import jax
import jax.numpy as jnp
from jax import lax
import numpy as np

D_MODEL = 1024
BATCH = 4
SEQ = 4096
DEPTH = 1
DEC_BATCH = 128
DEC_SEQ = 8
PAST_LEN = 2048
PAGE_SIZE = 128

ATT_GROUPS = ((128, 1), (512, 4), (2048, 16))
N_ATT_GROUPS = len(ATT_GROUPS)
HEAD_DIM = 64
HEADS_PER_GROUP = D_MODEL // 128
ATT_WIDTH = HEADS_PER_GROUP * HEAD_DIM
QKV_WIDTH = N_ATT_GROUPS * ATT_WIDTH
ATT_BLOCK = 128
ALIBI_MAX_EXP = 8.0

SSM_EXPAND = 2
D_INNER = SSM_EXPAND * D_MODEL
SSM_HEAD_DIM = 64
SSM_HEADS = D_INNER // SSM_HEAD_DIM
SSM_GROUPS = 4
D_STATE = 128
CONV_WIDTH = 4
CONV_DIM = D_INNER + 2 * SSM_GROUPS * D_STATE
SSD_CHUNK = 128

IN_SIZES = (QKV_WIDTH, QKV_WIDTH, QKV_WIDTH, ATT_WIDTH, D_INNER, CONV_DIM, SSM_HEADS, D_MODEL, D_MODEL)
IN_WIDTH = sum(IN_SIZES)
NORM_EPS = 1e-6
NEG_INF = -1e30

kernel_name = "griffin_dilated_attn_mamba2_step"


def _in_offsets():
    offs, acc = [], 0
    for s in IN_SIZES[:-1]:
        acc += s
        offs.append(acc)
    return offs


def _alibi_slopes():
    n = N_ATT_GROUPS * HEADS_PER_GROUP
    m = 2.0 ** (-ALIBI_MAX_EXP * np.arange(1, n + 1) / n)
    return jnp.asarray(m.reshape(N_ATT_GROUPS, HEADS_PER_GROUP), dtype=jnp.float32)


def _rmsnorm(x, g):
    xf = x.astype(jnp.float32)
    xf = xf * lax.rsqrt(jnp.mean(jnp.square(xf), axis=-1, keepdims=True) + NORM_EPS)
    return (xf * g.astype(jnp.float32)).astype(x.dtype)


def _front(x, c, norm_g, w_ada, b_ada, w_in):
    mod = jax.nn.silu(c) @ w_ada + b_ada
    shift, scale, gate = jnp.split(mod, 3, axis=-1)
    h = _rmsnorm(x, norm_g) * (1.0 + scale[:, None, :]) + shift[:, None, :]
    parts = jnp.split(h @ w_in, _in_offsets(), axis=-1)
    return parts, gate


def _split_heads(t):
    b, L, _ = t.shape
    return t.reshape(b, L, N_ATT_GROUPS, HEADS_PER_GROUP, HEAD_DIM)


def _dilated_attn_prompt(q, k, v, slopes, window, dil):
    b, T, H, hd = q.shape
    span = window // dil
    n_sub = -(-T // dil)
    n_pad = -(-n_sub // ATT_BLOCK) * ATT_BLOCK
    nb = n_pad // ATT_BLOCK
    pad = n_pad * dil - T

    def to_blocks(t):
        t = jnp.pad(t, ((0, 0), (0, pad), (0, 0), (0, 0)))
        return t.reshape(b, nb, ATT_BLOCK, dil, H, hd)

    def with_prev(t):
        prev = jnp.concatenate([jnp.zeros_like(t[:, :1]), t[:, :-1]], axis=1)
        return jnp.concatenate([prev, t], axis=2)

    qb = to_blocks(q)
    kk = with_prev(to_blocks(k))
    vv = with_prev(to_blocks(v))
    s = jnp.einsum('bnqrhd,bnkrhd->bnrhqk', qb, kk,
                   preferred_element_type=jnp.float32) * (hd ** -0.5)
    qi = jnp.arange(ATT_BLOCK)[:, None]
    kj = jnp.arange(2 * ATT_BLOCK)[None, :]
    delta = qi + ATT_BLOCK - kj
    blk = jnp.arange(nb)[:, None, None]
    valid = (delta >= 0) & (delta <= span) & ((blk > 0) | (kj >= ATT_BLOCK))
    alibi = -slopes[:, None, None] * (delta * dil).astype(jnp.float32)
    s = jnp.where(valid[None, :, None, None], s + alibi[None, None, None], NEG_INF)
    lse = jax.nn.logsumexp(s, axis=-1)
    p = jnp.exp(s - lse[..., None]).astype(v.dtype)
    o = jnp.einsum('bnrhqk,bnkrhd->bnqrhd', p, vv)
    o = o.reshape(b, n_pad * dil, H, hd)[:, :T]
    lse = lse.transpose(0, 1, 4, 2, 3).reshape(b, n_pad * dil, H)[:, :T]
    return o, lse


def _dilated_attn_sample(q, k_all, v_all, slopes, window, dil):
    b, S, H, hd = q.shape
    Lc = k_all.shape[1] - S
    span = window // dil
    j = jnp.arange(span + 1)
    idx = Lc + jnp.arange(S)[:, None] - dil * j[None, :]
    valid = idx >= 0
    idx = jnp.maximum(idx, 0)
    kg = jnp.take(k_all, idx, axis=1)
    vg = jnp.take(v_all, idx, axis=1)
    s = jnp.einsum('bshd,bsjhd->bhsj', q, kg,
                   preferred_element_type=jnp.float32) * (hd ** -0.5)
    s = s - slopes[:, None, None] * (dil * j).astype(jnp.float32)[None, None, :]
    s = jnp.where(valid[None, None], s, NEG_INF)
    lse = jax.nn.logsumexp(s, axis=-1)
    p = jnp.exp(s - lse[..., None]).astype(v_all.dtype)
    o = jnp.einsum('bhsj,bsjhd->bshd', p, vg)
    return o, lse.transpose(0, 2, 1)


def _combine_groups(outs, lses):
    w = jax.nn.softmax(jnp.stack(lses, 0), axis=0)
    o = jnp.einsum('gblh,gblhd->blhd', w.astype(outs[0].dtype), jnp.stack(outs, 0))
    return o.reshape(o.shape[0], o.shape[1], ATT_WIDTH)


def _causal_dwconv(xpad, w, bias):
    out = lax.conv_general_dilated(xpad, w[:, None, :], window_strides=(1,), padding='VALID',
                                   dimension_numbers=('NWC', 'WIO', 'NWC'),
                                   feature_group_count=xpad.shape[-1])
    return out + bias


def _ssd(x, dt, a, bm, cm, init_state, chunk):
    f32 = jnp.float32
    b, L, h, p = x.shape
    G, n = bm.shape[2], bm.shape[3]
    E = h // G
    nc = L // chunk
    xdt = (x.astype(f32) * dt[..., None]).reshape(b, nc, chunk, G, E, p)
    da = (dt * a).reshape(b, nc, chunk, G, E).transpose(0, 3, 4, 1, 2)
    bm = bm.astype(f32).reshape(b, nc, chunk, G, n)
    cm = cm.astype(f32).reshape(b, nc, chunk, G, n)
    da_cs = jnp.cumsum(da, axis=-1)
    causal = jnp.tril(jnp.ones((chunk, chunk), dtype=bool))
    seg = da_cs[..., :, None] - da_cs[..., None, :]
    decay = jnp.exp(jnp.where(causal, seg, -jnp.inf))
    cb = jnp.einsum('bclgn,bcsgn->bcgls', cm, bm)
    y_diag = jnp.einsum('bcgls,bgecls,bcsgep->bclgep', cb, decay, xdt)
    decay_to_end = jnp.exp(da_cs[..., -1:] - da_cs)
    states = jnp.einsum('bclgn,bgecl,bclgep->bcgepn', bm, decay_to_end, xdt)
    chunk_tot = da_cs[..., -1]

    def step(carry, inp):
        tot, st = inp
        return jnp.exp(tot)[..., None, None] * carry + st, carry

    init = init_state.astype(f32).reshape(b, G, E, p, n)
    final, prev = lax.scan(step, init, (jnp.moveaxis(chunk_tot, -1, 0), jnp.moveaxis(states, 1, 0)))
    prev = jnp.moveaxis(prev, 0, 1)
    y_off = jnp.einsum('bclgn,bcgepn,bgecl->bclgep', cm, prev, jnp.exp(da_cs))
    y = (y_diag + y_off).reshape(b, L, h, p)
    return y, final.reshape(b, h, p, n)


def _gated_rmsnorm(y, z, g):
    b, L, _ = y.shape
    u = (y * jax.nn.silu(z.astype(jnp.float32))).reshape(b, L, SSM_GROUPS, D_INNER // SSM_GROUPS)
    u = u * lax.rsqrt(jnp.mean(jnp.square(u), axis=-1, keepdims=True) + NORM_EPS)
    return u.reshape(b, L, D_INNER) * g.astype(jnp.float32)


def _ssm_branch(xpad, z, dt_raw, init_state, chunk, conv_w, conv_b, dt_bias, a_log, d_skip, ssm_norm_g):
    b, L = z.shape[:2]
    xbc = jax.nn.silu(_causal_dwconv(xpad, conv_w, conv_b))
    xs, bm, cm = jnp.split(xbc, [D_INNER, D_INNER + SSM_GROUPS * D_STATE], axis=-1)
    xs = xs.reshape(b, L, SSM_HEADS, SSM_HEAD_DIM)
    bm = bm.reshape(b, L, SSM_GROUPS, D_STATE)
    cm = cm.reshape(b, L, SSM_GROUPS, D_STATE)
    dt = jax.nn.softplus(dt_raw.astype(jnp.float32) + dt_bias.astype(jnp.float32))
    a = -jnp.exp(a_log.astype(jnp.float32))
    y, final = _ssd(xs, dt, a, bm, cm, init_state, chunk)
    y = y + d_skip.astype(jnp.float32)[:, None] * xs.astype(jnp.float32)
    y = _gated_rmsnorm(y.reshape(b, L, D_INNER), z, ssm_norm_g).astype(z.dtype)
    return y, final.astype(init_state.dtype)


def _back(x, gate, att, g_att, y_ssm, g_a, g_b, w_att_branch, w_ssm_branch, w_out):
    a_out = (att * jax.nn.silu(g_att)) @ w_att_branch
    m_out = y_ssm @ w_ssm_branch
    merged = jax.nn.sigmoid(g_a) * a_out + jax.nn.sigmoid(g_b) * m_out
    return x + gate[:, None, :] * (merged @ w_out)


def _prompt_layer(x, c, norm_g, w_ada, b_ada, w_in, conv_w, conv_b, dt_bias, a_log, d_skip,
                  ssm_norm_g, w_att_branch, w_ssm_branch, w_out):
    b, T, _ = x.shape
    (q, k, v, g_att, z, xbc, dt_raw, g_a, g_b), gate = _front(x, c, norm_g, w_ada, b_ada, w_in)
    q, k, v = _split_heads(q), _split_heads(k), _split_heads(v)
    slopes = _alibi_slopes()
    outs, lses, kv_states = [], [], []
    for g, (win, dil) in enumerate(ATT_GROUPS):
        o, lse = _dilated_attn_prompt(q[:, :, g], k[:, :, g], v[:, :, g], slopes[g], win, dil)
        outs.append(o)
        lses.append(lse)
        keep = min(win, T)
        kv_states.append(jnp.stack([k[:, T - keep:, g], v[:, T - keep:, g]], axis=2))
    att = _combine_groups(outs, lses)
    xpad = jnp.pad(xbc, ((0, 0), (CONV_WIDTH - 1, 0), (0, 0)))
    init = jnp.zeros((b, SSM_HEADS, SSM_HEAD_DIM, D_STATE), x.dtype)
    y_ssm, ssm_state = _ssm_branch(xpad, z, dt_raw, init, math.gcd(SSD_CHUNK, T), conv_w, conv_b,
                                   dt_bias, a_log, d_skip, ssm_norm_g)
    x = _back(x, gate, att, g_att, y_ssm, g_a, g_b, w_att_branch, w_ssm_branch, w_out)
    return x, kv_states, ssm_state, xpad[:, -(CONV_WIDTH - 1):]


def _sample_layer(x, c, kv0, kv1, kv2, ssm_state, conv_state, norm_g, w_ada, b_ada, w_in, conv_w,
                  conv_b, dt_bias, a_log, d_skip, ssm_norm_g, w_att_branch, w_ssm_branch, w_out):
    b, S, _ = x.shape
    (q, k, v, g_att, z, xbc, dt_raw, g_a, g_b), gate = _front(x, c, norm_g, w_ada, b_ada, w_in)
    q, k, v = _split_heads(q), _split_heads(k), _split_heads(v)
    slopes = _alibi_slopes()
    caches = (kv0, kv1, kv2)
    outs, lses, kv_states = [], [], []
    for g, (win, dil) in enumerate(ATT_GROUPS):
        kv_all = jnp.concatenate([caches[g], jnp.stack([k[:, :, g], v[:, :, g]], axis=2)], axis=1)
        o, lse = _dilated_attn_sample(q[:, :, g], kv_all[:, :, 0], kv_all[:, :, 1], slopes[g], win, dil)
        outs.append(o)
        lses.append(lse)
        kv_states.append(kv_all[:, -caches[g].shape[1]:])
    att = _combine_groups(outs, lses)
    xpad = jnp.concatenate([conv_state, xbc], axis=1)
    y_ssm, new_ssm = _ssm_branch(xpad, z, dt_raw, ssm_state, S, conv_w, conv_b,
                                 dt_bias, a_log, d_skip, ssm_norm_g)
    x = _back(x, gate, att, g_att, y_ssm, g_a, g_b, w_att_branch, w_ssm_branch, w_out)
    return x, kv_states, new_ssm, xpad[:, -(CONV_WIDTH - 1):]


def setup_inputs(seed: int = 0) -> dict:
    key = jax.random.key(seed)
    ks = jax.random.split(key, 24)
    f32 = jnp.float32

    def nrm(k, shape, scale=1.0):
        return jax.random.normal(k, shape, f32) * scale

    def kv_shape(win):
        return (DEPTH, DEC_BATCH, min(win, PAST_LEN), 2, HEADS_PER_GROUP, HEAD_DIM)

    dt0 = jnp.exp(jax.random.uniform(ks[13], (DEPTH, SSM_HEADS), f32, math.log(1e-3), math.log(1e-1)))
    return {
        "x_prompt": nrm(ks[0], (BATCH, SEQ, D_MODEL)),
        "x_sample": nrm(ks[1], (DEC_BATCH, DEC_SEQ, D_MODEL)),
        "c_prompt": nrm(ks[2], (BATCH, D_MODEL)),
        "c_sample": nrm(ks[3], (DEC_BATCH, D_MODEL)),
        "cache_kv_w128": nrm(ks[4], kv_shape(ATT_GROUPS[0][0])),
        "cache_kv_w512": nrm(ks[5], kv_shape(ATT_GROUPS[1][0])),
        "cache_kv_w2048": nrm(ks[6], kv_shape(ATT_GROUPS[2][0])),
        "state_ssm": nrm(ks[7], (DEPTH, DEC_BATCH, SSM_HEADS, SSM_HEAD_DIM, D_STATE), 0.1),
        "state_conv": nrm(ks[8], (DEPTH, DEC_BATCH, CONV_WIDTH - 1, CONV_DIM)),
        "norm_g": 1.0 + nrm(ks[9], (DEPTH, D_MODEL), 0.02),
        "w_ada": nrm(ks[10], (DEPTH, D_MODEL, 3 * D_MODEL), 0.5 * D_MODEL ** -0.5),
        "b_ada": nrm(ks[11], (DEPTH, 3 * D_MODEL), 0.02),
        "w_in": nrm(ks[12], (DEPTH, D_MODEL, IN_WIDTH), D_MODEL ** -0.5),
        "conv_w": nrm(ks[14], (DEPTH, CONV_WIDTH, CONV_DIM), CONV_WIDTH ** -0.5),
        "conv_b": nrm(ks[15], (DEPTH, CONV_DIM), 0.02),
        "dt_bias": dt0 + jnp.log(-jnp.expm1(-dt0)),
        "a_log": jnp.log(jax.random.uniform(ks[16], (DEPTH, SSM_HEADS), f32, 1.0, 16.0)),
        "d_skip": 1.0 + nrm(ks[17], (DEPTH, SSM_HEADS), 0.1),
        "ssm_norm_g": 1.0 + nrm(ks[18], (DEPTH, D_INNER), 0.02),
        "w_att_branch": nrm(ks[19], (DEPTH, ATT_WIDTH, D_MODEL), ATT_WIDTH ** -0.5),
        "w_ssm_branch": nrm(ks[20], (DEPTH, D_INNER, D_MODEL), D_INNER ** -0.5),
        "w_out": nrm(ks[21], (DEPTH, D_MODEL, D_MODEL), D_MODEL ** -0.5),
        "final_norm_g": 1.0 + nrm(ks[22], (D_MODEL,), 0.02),
    }


def reference(x_prompt, x_sample, c_prompt, c_sample, cache_kv_w128, cache_kv_w512, cache_kv_w2048,
              state_ssm, state_conv, norm_g, w_ada, b_ada, w_in, conv_w, conv_b, dt_bias, a_log,
              d_skip, ssm_norm_g, w_att_branch, w_ssm_branch, w_out, final_norm_g):
    xp, xs = x_prompt, x_sample
    kvp = ([], [], [])
    kvs = ([], [], [])
    ssm_p, conv_p, ssm_s, conv_s = [], [], [], []
    for l in range(DEPTH):
        xp, kv_new_p, sp, cp = _prompt_layer(
            xp, c_prompt, norm_g[l], w_ada[l], b_ada[l], w_in[l], conv_w[l], conv_b[l], dt_bias[l],
            a_log[l], d_skip[l], ssm_norm_g[l], w_att_branch[l], w_ssm_branch[l], w_out[l])
        xs, kv_new_s, ss, cs = _sample_layer(
            xs, c_sample, cache_kv_w128[l], cache_kv_w512[l], cache_kv_w2048[l], state_ssm[l],
            state_conv[l], norm_g[l], w_ada[l], b_ada[l], w_in[l], conv_w[l], conv_b[l], dt_bias[l],
            a_log[l], d_skip[l], ssm_norm_g[l], w_att_branch[l], w_ssm_branch[l], w_out[l])
        for g in range(N_ATT_GROUPS):
            kvp[g].append(kv_new_p[g])
            kvs[g].append(kv_new_s[g])
        ssm_p.append(sp)
        conv_p.append(cp)
        ssm_s.append(ss)
        conv_s.append(cs)
    y_prompt = _rmsnorm(xp, final_norm_g)
    y_sample = _rmsnorm(xs, final_norm_g)
    return (y_prompt, y_sample,
            jnp.stack(kvp[0]), jnp.stack(kvp[1]), jnp.stack(kvp[2]), jnp.stack(ssm_p), jnp.stack(conv_p),
            jnp.stack(kvs[0]), jnp.stack(kvs[1]), jnp.stack(kvs[2]), jnp.stack(ssm_s), jnp.stack(conv_s))
```

```python
import functools

import numpy as np
import jax
import jax.numpy as jnp
from jax import lax
from jax.experimental import pallas as pl
from jax.experimental.pallas import tpu as pltpu

F32 = jnp.float32
BF16 = jnp.bfloat16
HIGHEST = lax.Precision.HIGHEST

D_MODEL = 1024
ATT_GROUPS = ((128, 1), (512, 4), (2048, 16))
N_GROUPS = len(ATT_GROUPS)
HEAD_DIM = 64
HEADS = 8
ATT_WIDTH = HEADS * HEAD_DIM
QKV_WIDTH = N_GROUPS * ATT_WIDTH
ATT_BLOCK = 128
ALIBI_MAX_EXP = 8.0
D_INNER = 2048
SSM_HEAD_DIM = 64
SSM_HEADS = 32
SSM_GROUPS = 4
D_STATE = 128
CONV_WIDTH = 4
CONV_DIM = D_INNER + 2 * SSM_GROUPS * D_STATE
SSD_CHUNK = 128
IN_SIZES = (QKV_WIDTH, QKV_WIDTH, QKV_WIDTH, ATT_WIDTH, D_INNER, CONV_DIM, SSM_HEADS, D_MODEL, D_MODEL)
NORM_EPS = 1e-6
NEG_INF = -1e30

LANES = 128
P_WIDTH = 12288
COL_XBC = 0
COL_Q = 3072
COL_K = COL_Q + QKV_WIDTH
COL_V = COL_K + QKV_WIDTH
COL_GATT = COL_V + QKV_WIDTH
COL_Z = COL_GATT + ATT_WIDTH
COL_GA = COL_Z + D_INNER
COL_GB = COL_GA + D_MODEL
VMEM_LIMIT = 48 * 1024 * 1024


def _sigmoid(x):
    return 1.0 / (1.0 + jnp.exp(-x))


def _silu(x):
    return x * _sigmoid(x)


def _softplus(x):
    return jnp.maximum(x, 0.0) + jnp.log(1.0 + jnp.exp(-jnp.abs(x)))


def _dot(a, b):
    return jnp.dot(a, b, preferred_element_type=F32)


def _dot_nt(a, b, precision=None):
    return lax.dot_general(a, b, (((1,), (1,)), ((), ())), precision=precision,
                           preferred_element_type=F32)


def _pad_rows(x, rows):
    if x.shape[0] == rows:
        return x
    return jnp.concatenate([x, jnp.zeros((rows - x.shape[0],) + x.shape[1:], x.dtype)], axis=0)


def _params(sem):
    return pltpu.CompilerParams(dimension_semantics=sem, vmem_limit_bytes=VMEM_LIMIT)


def _mod_kernel(c_ref, w_ref, b_ref, o_ref):
    s = _silu(c_ref[...]).astype(BF16)
    o_ref[...] = _dot(s, w_ref[...].astype(BF16)) + b_ref[...]


def _modulation(c, w_ada, b_ada):
    n = c.shape[0]
    tn = 512
    return pl.pallas_call(
        _mod_kernel,
        grid=(3 * D_MODEL // tn,),
        in_specs=[pl.BlockSpec((n, D_MODEL), lambda j: (0, 0)),
                  pl.BlockSpec((D_MODEL, tn), lambda j: (0, j)),
                  pl.BlockSpec((1, tn), lambda j: (0, j))],
        out_specs=pl.BlockSpec((n, tn), lambda j: (0, j)),
        out_shape=jax.ShapeDtypeStruct((n, 3 * D_MODEL), F32),
        compiler_params=_params(("arbitrary",)),
        name="adaln_mod",
    )(c, w_ada, b_ada.reshape(1, -1))


def _front_kernel(x_ref, shift_ref, scale_ref, g_ref, w_ref, wdt_ref, out_ref, dt_ref, h_ref):
    bb, r, _ = x_ref.shape

    @pl.when(pl.program_id(2) == 0)
    def _():
        x = x_ref[...]
        ms = jnp.mean(x * x, axis=-1, keepdims=True)
        xn = x * lax.rsqrt(ms + NORM_EPS) * g_ref[...]
        h = xn * (1.0 + scale_ref[...]) + shift_ref[...]
        h2 = h.reshape(bb * r, D_MODEL).astype(BF16)
        h_ref[...] = h2
        dt_ref[...] = _dot(h2, wdt_ref[...]).reshape(bb, r, LANES)

    out_ref[...] = _dot(h_ref[...], w_ref[...]).reshape(out_ref.shape)


def _front(x, mod, norm_g, w_main, w_dt, bb, r, tn):
    nb, rr, _ = x.shape
    grid = (nb // bb, rr // r, P_WIDTH // tn)
    return pl.pallas_call(
        _front_kernel,
        grid=grid,
        in_specs=[pl.BlockSpec((bb, r, D_MODEL), lambda b, i, j: (b, i, 0)),
                  pl.BlockSpec((bb, 1, D_MODEL), lambda b, i, j: (b, 0, 0)),
                  pl.BlockSpec((bb, 1, D_MODEL), lambda b, i, j: (b, 0, 1)),
                  pl.BlockSpec((1, D_MODEL), lambda b, i, j: (0, 0)),
                  pl.BlockSpec((D_MODEL, tn), lambda b, i, j: (0, j)),
                  pl.BlockSpec((D_MODEL, LANES), lambda b, i, j: (0, 0))],
        out_specs=[pl.BlockSpec((bb, r, tn), lambda b, i, j: (b, i, j)),
                   pl.BlockSpec((bb, r, LANES), lambda b, i, j: (b, i, 0))],
        out_shape=[jax.ShapeDtypeStruct((nb, rr, P_WIDTH), F32),
                   jax.ShapeDtypeStruct((nb, rr, LANES), F32)],
        scratch_shapes=[pltpu.VMEM((bb * r, D_MODEL), BF16)],
        compiler_params=_params(("arbitrary", "arbitrary", "arbitrary")),
        name="front_proj",
    )(x, mod, mod, norm_g.reshape(1, -1), w_main, w_dt)


def _alibi_slopes():
    n = N_GROUPS * HEADS
    m = 2.0 ** (-ALIBI_MAX_EXP * np.arange(1, n + 1) / n)
    return m.reshape(N_GROUPS, HEADS).astype(np.float32)


def _prompt_bias(g):
    win, dil = ATT_GROUPS[g]
    span = win // dil
    qi = np.arange(ATT_BLOCK)[:, None]
    kj = np.arange(2 * ATT_BLOCK)[None, :]
    delta = qi + ATT_BLOCK - kj
    valid = (delta >= 0) & (delta <= span)
    slopes = _alibi_slopes()[g]
    alibi = -slopes[:, None, None] * (delta * dil).astype(np.float32)[None]
    later = np.where(valid[None], alibi, np.float32(NEG_INF))
    first = np.where((valid & (kj >= ATT_BLOCK))[None], alibi, np.float32(NEG_INF))
    return np.stack([first, later]).astype(np.float32)


def _attn_prompt_kernel(q_ref, kp_ref, kc_ref, vp_ref, vc_ref, bias_ref, o_ref, lse_ref):
    lo = lax.broadcasted_iota(jnp.int32, (ATT_BLOCK, LANES), 1) < HEAD_DIM
    for hp in range(HEADS // 2):
        sl = slice(hp * LANES, (hp + 1) * LANES)
        q = q_ref[0, :, sl] * (HEAD_DIM ** -0.5)
        k = jnp.concatenate([kp_ref[0, :, sl], kc_ref[0, :, sl]], axis=0).astype(BF16)
        v = jnp.concatenate([vp_ref[0, :, sl], vc_ref[0, :, sl]], axis=0).astype(BF16)
        outs, lses = [], []
        for half in range(2):
            qm = jnp.where(lo if half == 0 else jnp.logical_not(lo), q, 0.0).astype(BF16)
            s = _dot_nt(qm, k) + bias_ref[0, 2 * hp + half]
            m = jnp.max(s, axis=-1, keepdims=True)
            e = jnp.exp(s - m)
            l = jnp.sum(e, axis=-1, keepdims=True)
            outs.append(_dot(e.astype(BF16), v) * (1.0 / l))
            lses.append(m + jnp.log(l))
        o_ref[0, :, sl] = jnp.where(lo, outs[0], outs[1])
        lse_ref[0, :, sl] = jnp.where(lo, lses[0], lses[1])


def _attn_prompt(P, g):
    B, T, _ = P.shape
    _, dil = ATT_GROUPS[g]
    n_sub = T // dil
    nblk = n_sub // ATT_BLOCK
    Pv = P.reshape(B, n_sub, dil * P_WIDTH)
    per_row = P_WIDTH // ATT_WIDTH
    cq, ck, cv = (COL_Q // ATT_WIDTH + g, COL_K // ATT_WIDTH + g, COL_V // ATT_WIDTH + g)
    blk_spec = lambda col, prev: pl.BlockSpec(
        (1, ATT_BLOCK, ATT_WIDTH),
        (lambda n, b, r: (b, jnp.maximum(n - 1, 0), r * per_row + col)) if prev
        else (lambda n, b, r: (b, n, r * per_row + col)))
    out_spec = pl.BlockSpec((1, ATT_BLOCK, ATT_WIDTH), lambda n, b, r: (b, n, r))
    bias = jnp.asarray(_prompt_bias(g))
    o, lse = pl.pallas_call(
        _attn_prompt_kernel,
        grid=(nblk, B, dil),
        in_specs=[blk_spec(cq, False), blk_spec(ck, True), blk_spec(ck, False),
                  blk_spec(cv, True), blk_spec(cv, False),
                  pl.BlockSpec((1, HEADS, ATT_BLOCK, 2 * ATT_BLOCK),
                               lambda n, b, r: (jnp.minimum(n, 1), 0, 0, 0))],
        out_specs=[out_spec, out_spec],
        out_shape=[jax.ShapeDtypeStruct((B, n_sub, dil * ATT_WIDTH), F32)] * 2,
        compiler_params=_params(("arbitrary", "arbitrary", "arbitrary")),
        name=f"attn_prompt_g{g}",
    )(Pv, Pv, Pv, Pv, Pv, bias)
    return o.reshape(B, T, ATT_WIDTH), lse.reshape(B, T, ATT_WIDTH)


def _sample_bias(g, S):
    win, dil = ATT_GROUPS[g]
    slopes = _alibi_slopes()[g]
    s_idx = np.tile(np.arange(S), HEADS)[:, None]
    slope = np.repeat(slopes, S)[:, None]

    def bias(pos, extra_valid):
        dist = win + s_idx - pos[None, :]
        valid = (dist >= 0) & (dist <= win) & (dist % dil == 0) & extra_valid[None, :]
        return np.where(valid, -slope * dist.astype(np.float32), np.float32(NEG_INF)).astype(np.float32)

    main = bias(np.arange(win) + S, np.ones(win, bool))
    head = bias(np.arange(LANES), np.arange(LANES) < S)
    return main, head


def _attn_sample_kernel(S, cache_ref, q_ref, k_ref, v_ref, bm_ref, bh_ref, mask_ref,
                        new_ref, o_ref, lse_ref):
    w = cache_ref.shape[1]
    new_ref[0, 0:w - S, :] = cache_ref[0, S:w, :]
    new_ref[0, w - S:w, 0:ATT_WIDTH] = k_ref[0]
    new_ref[0, w - S:w, ATT_WIDTH:2 * ATT_WIDTH] = v_ref[0]

    mask = mask_ref[...]
    q = q_ref[0] * (HEAD_DIM ** -0.5)
    qbd = (jnp.concatenate([q] * HEADS, axis=0) * mask).astype(BF16)
    k_main = new_ref[0, :, 0:ATT_WIDTH].astype(BF16)
    v_main = new_ref[0, :, ATT_WIDTH:2 * ATT_WIDTH].astype(BF16)
    k_head = cache_ref[0, 0:LANES, 0:ATT_WIDTH].astype(BF16)
    v_head = cache_ref[0, 0:LANES, ATT_WIDTH:2 * ATT_WIDTH].astype(BF16)
    s_main = _dot_nt(qbd, k_main) + bm_ref[...]
    s_head = _dot_nt(qbd, k_head) + bh_ref[...]
    m = jnp.maximum(jnp.max(s_main, axis=-1, keepdims=True), jnp.max(s_head, axis=-1, keepdims=True))
    e_main = jnp.exp(s_main - m)
    e_head = jnp.exp(s_head - m)
    l = jnp.sum(e_main, axis=-1, keepdims=True) + jnp.sum(e_head, axis=-1, keepdims=True)
    o = (_dot(e_main.astype(BF16), v_main) + _dot(e_head.astype(BF16), v_head)) * (1.0 / l)
    o = o * mask
    lse = (m + jnp.log(l)) * mask
    o_acc, lse_acc = o[0:S], lse[0:S]
    for h in range(1, HEADS):
        o_acc = o_acc + o[h * S:(h + 1) * S]
        lse_acc = lse_acc + lse[h * S:(h + 1) * S]
    o_ref[0] = o_acc
    lse_ref[0] = lse_acc


def _attn_sample(Ps, cache, g):
    B, S, _ = Ps.shape
    w = cache.shape[1]
    cache2 = cache.reshape(B, w, 2 * ATT_WIDTH)
    bm, bh = _sample_bias(g, S)
    mask = (np.arange(HEADS * S)[:, None] // S == np.arange(ATT_WIDTH)[None, :] // HEAD_DIM).astype(np.float32)
    col = lambda c: pl.BlockSpec((1, S, ATT_WIDTH), lambda b: (b, 0, c // ATT_WIDTH + g))
    const = lambda a: pl.BlockSpec(a.shape, lambda b: (0, 0))
    out_spec = pl.BlockSpec((1, S, ATT_WIDTH), lambda b: (b, 0, 0))
    new, o, lse = pl.pallas_call(
        functools.partial(_attn_sample_kernel, S),
        grid=(B,),
        in_specs=[pl.BlockSpec((1, w, 2 * ATT_WIDTH), lambda b: (b, 0, 0)),
                  col(COL_Q), col(COL_K), col(COL_V), const(bm), const(bh), const(mask)],
        out_specs=[pl.BlockSpec((1, w, 2 * ATT_WIDTH), lambda b: (b, 0, 0)), out_spec, out_spec],
        out_shape=[jax.ShapeDtypeStruct((B, w, 2 * ATT_WIDTH), F32),
                   jax.ShapeDtypeStruct((B, S, ATT_WIDTH), F32),
                   jax.ShapeDtypeStruct((B, S, ATT_WIDTH), F32)],
        compiler_params=_params(("arbitrary",)),
        name=f"attn_sample_g{g}",
    )(cache2, Ps, Ps, Ps, jnp.asarray(bm), jnp.asarray(bh), jnp.asarray(mask))
    return new.reshape(cache.shape), o, lse


def _ssd_kernel(L, xbc_ref, z_ref, dt_ref, convin_ref, statein_ref, convw_ref, convb_ref,
                dtb_ref, a_ref, dskip_ref, ng_ref, y_ref, state_ref, xpad_s, act_s, y_s, xw_s):
    LP = SSD_CHUNK
    pad = CONV_WIDTH - 1
    base = 8

    @pl.when(pl.program_id(1) == 0)
    def _():
        state_ref[...] = statein_ref[...]
        xpad_s[base - pad:base, :] = convin_ref[0]

    xpad_s[base:base + L, :] = xbc_ref[0]
    cw = 512
    for cc in range(CONV_DIM // cw):
        cl = slice(cc * cw, (cc + 1) * cw)
        conv = convb_ref[:, cl] + convw_ref[pad:pad + 1, cl] * xpad_s[base:base + L, cl]
        for i in range(pad):
            conv = conv + convw_ref[i:i + 1, cl] * xpad_s[base - pad + i:base - pad + i + L, cl]
        act_s[:, cl] = _silu(conv)
    xpad_s[base - pad:base, :] = xbc_ref[0, L - pad:L, :]

    dt = _softplus(dt_ref[0] + dtb_ref[...])
    da = _pad_rows(dt * a_ref[...], LP)
    row = lax.broadcasted_iota(jnp.int32, (LP, LP), 0)
    colm = lax.broadcasted_iota(jnp.int32, (LP, LP), 1)
    tril = (row >= colm).astype(F32)
    eye = (row == colm).astype(F32)
    cs = jnp.dot(tril, da, precision=HIGHEST, preferred_element_type=F32)
    cs_t = _dot_nt(eye, cs, precision=HIGHEST)
    cs_l = cs[0:L]
    cs_last = cs[LP - 1:LP]
    ecs = jnp.exp(cs_l)
    dte = jnp.exp(cs_last - cs_l)
    etot = jnp.exp(cs_last)
    causal = row[0:L] >= colm[0:L]
    lo = lax.broadcasted_iota(jnp.int32, (L, LANES), 1) < SSM_HEAD_DIM

    def pair(v, ha):
        return jnp.where(lo, v[:, ha:ha + 1], v[:, ha + 1:ha + 2])

    hpg = SSM_HEADS // SSM_GROUPS
    gw = hpg * SSM_HEAD_DIM
    for g in range(SSM_GROUPS):
        bg = act_s[:, D_INNER + g * D_STATE:D_INNER + (g + 1) * D_STATE]
        cg = act_s[:, D_INNER + SSM_GROUPS * D_STATE + g * D_STATE:
                   D_INNER + SSM_GROUPS * D_STATE + (g + 1) * D_STATE]
        bg_pad = _pad_rows(bg, LP).astype(BF16)
        cb = _dot_nt(cg.astype(BF16), bg_pad) if L >= 16 else _dot_nt(cg, _pad_rows(bg, LP))
        sg = state_ref[0, g * gw:(g + 1) * gw, :]
        yoff_g = _dot_nt(cg.astype(BF16), sg.astype(BF16)) if L >= 16 else _dot_nt(cg, sg)
        for hq in range(hpg // 2):
            ha = g * hpg + 2 * hq
            sl = slice(ha * SSM_HEAD_DIM, (ha + 2) * SSM_HEAD_DIM)
            xdt = act_s[:, sl] * pair(dt, ha)
            xdt_pad = _pad_rows(xdt, LP).astype(BF16)
            ys = []
            for h in (ha, ha + 1):
                seg = cs_l[:, h:h + 1] - cs_t[h:h + 1, :]
                m = cb * jnp.exp(jnp.where(causal, seg, -jnp.inf))
                ys.append(_dot(m.astype(BF16), xdt_pad) if L >= 16 else _dot(m, xdt_pad.astype(F32)))
            ydiag = jnp.where(lo, ys[0], ys[1])
            yoff = yoff_g[:, 2 * hq * SSM_HEAD_DIM:(2 * hq + 2) * SSM_HEAD_DIM] * pair(ecs, ha)
            y_s[:, sl] = ydiag + yoff
            xw_s[0:L, 2 * hq * SSM_HEAD_DIM:(2 * hq + 2) * SSM_HEAD_DIM] = xdt * pair(dte, ha)
        if L < LP:
            xw_s[L:LP, :] = jnp.zeros((LP - L, gw), F32)
        upd = _dot(xw_s[...].T.astype(BF16), bg_pad)
        for hh in range(hpg):
            h = g * hpg + hh
            rows = slice(h * SSM_HEAD_DIM, (h + 1) * SSM_HEAD_DIM)
            urows = slice(hh * SSM_HEAD_DIM, (hh + 1) * SSM_HEAD_DIM)
            state_ref[0, rows, :] = state_ref[0, rows, :] * etot[0:1, h:h + 1] + upd[urows]

    for g in range(SSM_GROUPS):
        gl = slice(g * gw, (g + 1) * gw)
        yg = y_s[:, gl] + dskip_ref[:, gl] * act_s[:, gl]
        ug = yg * _silu(z_ref[0, :, gl])
        ms = jnp.mean(ug * ug, axis=-1, keepdims=True)
        y_ref[0, :, gl] = ug * lax.rsqrt(ms + NORM_EPS) * ng_ref[:, gl]


def _ssd(P, dt, conv_in, state_in, L, conv_w, conv_b, dt_bias, a_log, d_skip, ssm_norm_g):
    B, T, _ = P.shape
    nc = T // L
    pad_lanes = lambda v: jnp.pad(v.astype(F32), (0, LANES - SSM_HEADS)).reshape(1, LANES)
    a_neg = pad_lanes(-jnp.exp(a_log.astype(F32)))
    dskip = jnp.repeat(d_skip.astype(F32), SSM_HEAD_DIM).reshape(1, D_INNER)
    const = lambda shape: pl.BlockSpec(shape, lambda b, c: (0, 0))
    y, state = pl.pallas_call(
        functools.partial(_ssd_kernel, L),
        grid=(B, nc),
        in_specs=[pl.BlockSpec((1, L, CONV_DIM), lambda b, c: (b, c, COL_XBC // CONV_DIM)),
                  pl.BlockSpec((1, L, D_INNER), lambda b, c: (b, c, COL_Z // D_INNER)),
                  pl.BlockSpec((1, L, LANES), lambda b, c: (b, c, 0)),
                  pl.BlockSpec((1, CONV_WIDTH - 1, CONV_DIM), lambda b, c: (b, 0, 0)),
                  pl.BlockSpec((1, D_INNER, D_STATE), lambda b, c: (b, 0, 0)),
                  const((CONV_WIDTH, CONV_DIM)), const((1, CONV_DIM)), const((1, LANES)),
                  const((1, LANES)), const((1, D_INNER)), const((1, D_INNER))],
        out_specs=[pl.BlockSpec((1, L, D_INNER), lambda b, c: (b, c, 0)),
                   pl.BlockSpec((1, D_INNER, D_STATE), lambda b, c: (b, 0, 0))],
        out_shape=[jax.ShapeDtypeStruct((B, T, D_INNER), F32),
                   jax.ShapeDtypeStruct((B, D_INNER, D_STATE), F32)],
        scratch_shapes=[pltpu.VMEM((8 + L, CONV_DIM), F32),
                        pltpu.VMEM((L, CONV_DIM), F32),
                        pltpu.VMEM((L, D_INNER), F32),
                        pltpu.VMEM((SSD_CHUNK, D_INNER // SSM_GROUPS), F32)],
        compiler_params=_params(("arbitrary", "arbitrary")),
        name=f"ssd_L{L}",
    )(P, P, dt, conv_in, state_in.reshape(B, D_INNER, D_STATE), conv_w, conv_b.reshape(1, -1),
      pad_lanes(dt_bias), a_neg, dskip, ssm_norm_g.reshape(1, -1))
    return y, state.reshape(state_in.shape)


def _back_kernel(o0_ref, o1_ref, o2_ref, l0_ref, l1_ref, l2_ref, gatt_ref, yssm_ref, ga_ref, gb_ref,
                 x_ref, gate_ref, watt_ref, wssm_ref, wout_ref, fg_ref, y_ref):
    bb, r, _ = x_ref.shape
    n = bb * r
    flat = lambda ref: ref[...].reshape(n, ref.shape[-1])
    l0, l1, l2 = flat(l0_ref), flat(l1_ref), flat(l2_ref)
    m = jnp.maximum(jnp.maximum(l0, l1), l2)
    e0, e1, e2 = jnp.exp(l0 - m), jnp.exp(l1 - m), jnp.exp(l2 - m)
    att = (e0 * flat(o0_ref) + e1 * flat(o1_ref) + e2 * flat(o2_ref)) * (1.0 / (e0 + e1 + e2))
    a_out = _dot((att * _silu(flat(gatt_ref))).astype(BF16), watt_ref[...])
    m_out = _dot(flat(yssm_ref).astype(BF16), wssm_ref[...])
    merged = _sigmoid(flat(ga_ref)) * a_out + _sigmoid(flat(gb_ref)) * m_out
    res = _dot(merged.astype(BF16), wout_ref[...]).reshape(bb, r, D_MODEL)
    xo = x_ref[...] + gate_ref[...] * res
    ms = jnp.mean(xo * xo, axis=-1, keepdims=True)
    y_ref[...] = xo * lax.rsqrt(ms + NORM_EPS) * fg_ref[...]


def _back(x, mod, P, outs, lses, y_ssm, w_att, w_ssm, w_out, final_g, bb, r):
    nb, rr, _ = x.shape
    grid = (nb // bb, rr // r)
    row = lambda width, col: pl.BlockSpec((bb, r, width), lambda b, i: (b, i, col // width))
    const = lambda a: pl.BlockSpec(a.shape, lambda b, i: (0,) * a.ndim)
    fg = final_g.reshape(1, -1)
    return pl.pallas_call(
        _back_kernel,
        grid=grid,
        in_specs=[row(ATT_WIDTH, 0)] * 6
        + [row(ATT_WIDTH, COL_GATT), row(D_INNER, 0), row(D_MODEL, COL_GA), row(D_MODEL, COL_GB),
           row(D_MODEL, 0), pl.BlockSpec((bb, 1, D_MODEL), lambda b, i: (b, 0, 2)),
           const(w_att), const(w_ssm), const(w_out), const(fg)],
        out_specs=row(D_MODEL, 0),
        out_shape=jax.ShapeDtypeStruct(x.shape, F32),
        compiler_params=_params(("arbitrary", "arbitrary")),
        name="back_proj",
    )(*outs, *lses, P, y_ssm, P, P, x, mod, w_att, w_ssm, w_out, fg)


def kernel(x_prompt, x_sample, c_prompt, c_sample, cache_kv_w128, cache_kv_w512, cache_kv_w2048,
           state_ssm, state_conv, norm_g, w_ada, b_ada, w_in, conv_w, conv_b, dt_bias, a_log,
           d_skip, ssm_norm_g, w_att_branch, w_ssm_branch, w_out, final_norm_g):
    depth = w_in.shape[0]
    assert depth == 1
    B, T, _ = x_prompt.shape
    Bs, S, _ = x_sample.shape
    caches = (cache_kv_w128, cache_kv_w512, cache_kv_w2048)
    l = 0

    offs = np.cumsum((0,) + IN_SIZES)
    sec = lambda i: w_in[l][:, offs[i]:offs[i + 1]]
    w_main = jnp.concatenate([sec(5), sec(0), sec(1), sec(2), sec(3), sec(4), sec(7), sec(8)], axis=1).astype(BF16)
    w_dt = jnp.pad(sec(6), ((0, 0), (0, LANES - SSM_HEADS))).astype(BF16)
    w_att = w_att_branch[l].astype(BF16)
    w_ssm = w_ssm_branch[l].astype(BF16)
    w_o = w_out[l].astype(BF16)

    n_pad = -(B + Bs) % 8
    c_all = jnp.concatenate([c_prompt, c_sample, jnp.zeros((n_pad, D_MODEL), F32)], axis=0)
    mod = _modulation(c_all, w_ada[l], b_ada[l])
    mod_p = mod[:B].reshape(B, 1, 3 * D_MODEL)
    mod_s = mod[B:B + Bs].reshape(Bs, 1, 3 * D_MODEL)

    ssm_args = (conv_w[l], conv_b[l], dt_bias[l], a_log[l], d_skip[l], ssm_norm_g[l])

    Pp, dtp = _front(x_prompt, mod_p, norm_g[l], w_main, w_dt, bb=1, r=1024, tn=1024)
    outs, lses, kv_p = [], [], []
    for g, (win, _) in enumerate(ATT_GROUPS):
        o, lse = _attn_prompt(Pp, g)
        outs.append(o)
        lses.append(lse)
        keep = min(win, T)
        kk = Pp[:, T - keep:, COL_K + g * ATT_WIDTH:COL_K + (g + 1) * ATT_WIDTH]
        vv = Pp[:, T - keep:, COL_V + g * ATT_WIDTH:COL_V + (g + 1) * ATT_WIDTH]
        kv_p.append(jnp.stack([kk.reshape(B, keep, HEADS, HEAD_DIM), vv.reshape(B, keep, HEADS, HEAD_DIM)],
                              axis=2)[None])
    zeros_state = jnp.zeros((B, SSM_HEADS, SSM_HEAD_DIM, D_STATE), F32)
    zeros_conv = jnp.zeros((B, CONV_WIDTH - 1, CONV_DIM), F32)
    y_ssm_p, ssm_p = _ssd(Pp, dtp, zeros_conv, zeros_state, SSD_CHUNK, *ssm_args)
    conv_p = Pp[:, T - (CONV_WIDTH - 1):, COL_XBC:COL_XBC + CONV_DIM]
    y_prompt = _back(x_prompt, mod_p, Pp, outs, lses, y_ssm_p, w_att, w_ssm, w_o, final_norm_g, bb=1, r=256)

    Ps, dts = _front(x_sample, mod_s, norm_g[l], w_main, w_dt, bb=32, r=S, tn=1024)
    outs, lses, kv_s = [], [], []
    for g in range(N_GROUPS):
        new, o, lse = _attn_sample(Ps, caches[g][l], g)
        outs.append(o)
        lses.append(lse)
        kv_s.append(new[None])
    y_ssm_s, ssm_s = _ssd(Ps, dts, state_conv[l], state_ssm[l], S, *ssm_args)
    conv_s = jnp.concatenate([state_conv[l], Ps[:, :, COL_XBC:COL_XBC + CONV_DIM]], axis=1)[:, -(CONV_WIDTH - 1):]
    y_sample = _back(x_sample, mod_s, Ps, outs, lses, y_ssm_s, w_att, w_ssm, w_o, final_norm_g, bb=32, r=S)

    return (y_prompt, y_sample, kv_p[0], kv_p[1], kv_p[2], ssm_p[None], conv_p[None],
            kv_s[0], kv_s[1], kv_s[2], ssm_s[None], conv_s[None])
```

```python
import functools

import numpy as np
import jax
import jax.numpy as jnp
from jax import lax
from jax.experimental import pallas as pl
from jax.experimental.pallas import tpu as pltpu

F32 = jnp.float32
BF16 = jnp.bfloat16
HIGHEST = lax.Precision.HIGHEST

D_MODEL = 1024
ATT_GROUPS = ((128, 1), (512, 4), (2048, 16))
N_GROUPS = len(ATT_GROUPS)
HEAD_DIM = 64
HEADS = 8
ATT_WIDTH = HEADS * HEAD_DIM
QKV_WIDTH = N_GROUPS * ATT_WIDTH
ATT_BLOCK = 128
ALIBI_MAX_EXP = 8.0
D_INNER = 2048
SSM_HEAD_DIM = 64
SSM_HEADS = 32
SSM_GROUPS = 4
D_STATE = 128
CONV_WIDTH = 4
CONV_DIM = D_INNER + 2 * SSM_GROUPS * D_STATE
SSD_CHUNK = 128
IN_SIZES = (QKV_WIDTH, QKV_WIDTH, QKV_WIDTH, ATT_WIDTH, D_INNER, CONV_DIM, SSM_HEADS, D_MODEL, D_MODEL)
NORM_EPS = 1e-6
NEG_INF = -1e30

LANES = 128
P_WIDTH = 12288
COL_XBC = 0
COL_Q = 3072
COL_K = COL_Q + QKV_WIDTH
COL_V = COL_K + QKV_WIDTH
COL_GATT = COL_V + QKV_WIDTH
COL_Z = COL_GATT + ATT_WIDTH
COL_GA = COL_Z + D_INNER
COL_GB = COL_GA + D_MODEL
VMEM_LIMIT = 48 * 1024 * 1024


def _sigmoid(x):
    return 1.0 / (1.0 + jnp.exp(-x))


def _silu(x):
    return x * _sigmoid(x)


def _softplus(x):
    return jnp.maximum(x, 0.0) + jnp.log(1.0 + jnp.exp(-jnp.abs(x)))


def _dot(a, b):
    return jnp.dot(a, b, preferred_element_type=F32)


def _dot_nt(a, b, precision=None):
    return lax.dot_general(a, b, (((1,), (1,)), ((), ())), precision=precision,
                           preferred_element_type=F32)


def _pad_rows(x, rows):
    if x.shape[0] == rows:
        return x
    return jnp.concatenate([x, jnp.zeros((rows - x.shape[0],) + x.shape[1:], x.dtype)], axis=0)


def _params(sem):
    return pltpu.CompilerParams(dimension_semantics=sem, vmem_limit_bytes=VMEM_LIMIT)


def _mod_kernel(c_ref, w_ref, b_ref, o_ref):
    s = _silu(c_ref[...]).astype(BF16)
    o_ref[...] = _dot(s, w_ref[...].astype(BF16)) + b_ref[...]


def _modulation(c, w_ada, b_ada):
    n = c.shape[0]
    tn = 512
    return pl.pallas_call(
        _mod_kernel,
        grid=(3 * D_MODEL // tn,),
        in_specs=[pl.BlockSpec((n, D_MODEL), lambda j: (0, 0)),
                  pl.BlockSpec((D_MODEL, tn), lambda j: (0, j)),
                  pl.BlockSpec((1, tn), lambda j: (0, j))],
        out_specs=pl.BlockSpec((n, tn), lambda j: (0, j)),
        out_shape=jax.ShapeDtypeStruct((n, 3 * D_MODEL), F32),
        compiler_params=_params(("arbitrary",)),
        name="adaln_mod",
    )(c, w_ada, b_ada.reshape(1, -1))


def _front_kernel(x_ref, shift_ref, scale_ref, g_ref, w_ref, wdt_ref, out_ref, dt_ref, h_ref):
    bb, r, _ = x_ref.shape

    @pl.when(pl.program_id(2) == 0)
    def _():
        x = x_ref[...]
        ms = jnp.mean(x * x, axis=-1, keepdims=True)
        xn = x * lax.rsqrt(ms + NORM_EPS) * g_ref[...]
        h = xn * (1.0 + scale_ref[...]) + shift_ref[...]
        h2 = h.reshape(bb * r, D_MODEL).astype(BF16)
        h_ref[...] = h2
        dt_ref[...] = _dot(h2, wdt_ref[...]).reshape(bb, r, LANES)

    out_ref[...] = _dot(h_ref[...], w_ref[...]).reshape(out_ref.shape)


def _front(x, mod, norm_g, w_main, w_dt, bb, r, tn):
    nb, rr, _ = x.shape
    grid = (nb // bb, rr // r, P_WIDTH // tn)
    return pl.pallas_call(
        _front_kernel,
        grid=grid,
        in_specs=[pl.BlockSpec((bb, r, D_MODEL), lambda b, i, j: (b, i, 0)),
                  pl.BlockSpec((bb, 1, D_MODEL), lambda b, i, j: (b, 0, 0)),
                  pl.BlockSpec((bb, 1, D_MODEL), lambda b, i, j: (b, 0, 1)),
                  pl.BlockSpec((1, D_MODEL), lambda b, i, j: (0, 0)),
                  pl.BlockSpec((D_MODEL, tn), lambda b, i, j: (0, j)),
                  pl.BlockSpec((D_MODEL, LANES), lambda b, i, j: (0, 0))],
        out_specs=[pl.BlockSpec((bb, r, tn), lambda b, i, j: (b, i, j)),
                   pl.BlockSpec((bb, r, LANES), lambda b, i, j: (b, i, 0))],
        out_shape=[jax.ShapeDtypeStruct((nb, rr, P_WIDTH), F32),
                   jax.ShapeDtypeStruct((nb, rr, LANES), F32)],
        scratch_shapes=[pltpu.VMEM((bb * r, D_MODEL), BF16)],
        compiler_params=_params(("arbitrary", "arbitrary", "arbitrary")),
        name="front_proj",
    )(x, mod, mod, norm_g.reshape(1, -1), w_main, w_dt)


def _alibi_slopes():
    n = N_GROUPS * HEADS
    m = 2.0 ** (-ALIBI_MAX_EXP * np.arange(1, n + 1) / n)
    return m.reshape(N_GROUPS, HEADS).astype(np.float32)


def _prompt_bias(g):
    win, dil = ATT_GROUPS[g]
    span = win // dil
    qi = np.arange(ATT_BLOCK)[:, None]
    kj = np.arange(2 * ATT_BLOCK)[None, :]
    delta = qi + ATT_BLOCK - kj
    valid = (delta >= 0) & (delta <= span)
    slopes = _alibi_slopes()[g]
    alibi = -slopes[:, None, None] * (delta * dil).astype(np.float32)[None]
    later = np.where(valid[None], alibi, np.float32(NEG_INF))
    first = np.where((valid & (kj >= ATT_BLOCK))[None], alibi, np.float32(NEG_INF))
    return np.stack([first, later]).astype(np.float32)


ATT_TILE = ATT_BLOCK * max(d for _, d in ATT_GROUPS)
ATT_UNROLL = 8


def _attn_pair(q, k, v, bias, lo):
    outs, lses = [], []
    for half in range(2):
        qm = jnp.where(lo if half == 0 else jnp.logical_not(lo), q, 0.0).astype(BF16)
        s = _dot_nt(qm, k) + bias(half)
        m = jnp.max(s, axis=-1, keepdims=True)
        e = jnp.exp(s - m)
        l = jnp.sum(e, axis=-1, keepdims=True)
        outs.append(_dot(e.astype(BF16), v) * (1.0 / l))
        lses.append(m + jnp.log(l))
    return jnp.where(lo, outs[0], outs[1]), jnp.where(lo, lses[0], lses[1])


def _attn_prompt_kernel(*refs):
    ins, (bias_ref, att_ref, o_s, lse_s) = refs[:5 * N_GROUPS], refs[5 * N_GROUPS:]
    lo = lax.broadcasted_iota(jnp.int32, (ATT_BLOCK, LANES), 1) < HEAD_DIM
    scale = HEAD_DIM ** -0.5
    lead_var = jnp.minimum(pl.program_id(2), 1)

    for g, (_, d) in enumerate(ATT_GROUPS):
        q_ref, kp_ref, kc_ref, vp_ref, vc_ref = ins[5 * g:5 * g + 5]
        ext = ATT_BLOCK * d
        nb = ATT_TILE // ext

        def rows(start, n, d=d):
            return pl.ds(start, n) if d == 1 else pl.ds(start, n, stride=d)

        def emit(start, o, lse, g=g, rows=rows):
            o_s[g, rows(start, ATT_BLOCK), :] = o
            lse_s[g, rows(start, ATT_BLOCK), :] = lse

        def lead(r, c, g=g, rows=rows, emit=emit, q_ref=q_ref, kp_ref=kp_ref, kc_ref=kc_ref,
                 vp_ref=vp_ref, vc_ref=vc_ref):
            sel = rows(r, ATT_BLOCK)
            k = jnp.concatenate([kp_ref[0, sel, :], kc_ref[0, sel, :]], axis=0).astype(BF16)
            v = jnp.concatenate([vp_ref[0, sel, :], vc_ref[0, sel, :]], axis=0).astype(BF16)
            o, lse = _attn_pair(q_ref[0, sel, :] * scale, k, v,
                                lambda half: bias_ref[g, lead_var, half], lo)
            emit(r, o, lse)
            return c

        def later(n, c, g=g, rows=rows, emit=emit, ext=ext, nb=nb, q_ref=q_ref, kc_ref=kc_ref, vc_ref=vc_ref):
            if d == 1:
                prev = pl.multiple_of(ext * n, ATT_BLOCK)
            else:
                r = n // (nb - 1)
                prev = r + ext * (n - r * (nb - 1))
            k = kc_ref[0, rows(prev, 2 * ATT_BLOCK), :].astype(BF16)
            v = vc_ref[0, rows(prev, 2 * ATT_BLOCK), :].astype(BF16)
            o, lse = _attn_pair(q_ref[0, rows(prev + ext, ATT_BLOCK), :] * scale, k, v,
                                lambda half: bias_ref[g, 1, half], lo)
            emit(prev + ext, o, lse)
            return c

        lax.fori_loop(0, d, lead, 0, unroll=min(d, ATT_UNROLL))
        if nb > 1:
            trips = d * (nb - 1)
            lax.fori_loop(0, trips, later, 0, unroll=max(u for u in range(1, ATT_UNROLL + 2) if trips % u == 0))

    cr = 256

    def combine(i, c):
        sel = pl.ds(pl.multiple_of(i * cr, cr), cr)
        l0, l1, l2 = lse_s[0, sel, :], lse_s[1, sel, :], lse_s[2, sel, :]
        m = jnp.maximum(jnp.maximum(l0, l1), l2)
        e0, e1, e2 = jnp.exp(l0 - m), jnp.exp(l1 - m), jnp.exp(l2 - m)
        att_ref[0, sel, :] = ((e0 * o_s[0, sel, :] + e1 * o_s[1, sel, :] + e2 * o_s[2, sel, :])
                              * (1.0 / (e0 + e1 + e2)))
        return c

    lax.fori_loop(0, ATT_TILE // cr, combine, 0)


def _attn_prompt(P):
    B, T, _ = P.shape
    assert T % ATT_TILE == 0
    pairs = ATT_WIDTH // LANES
    in_specs, args = [], []
    for g, (_, d) in enumerate(ATT_GROUPS):
        ext = ATT_BLOCK * d
        nb = ATT_TILE // ext

        def cur(col, g=g):
            c0 = (col + g * ATT_WIDTH) // LANES
            return pl.BlockSpec((1, ATT_TILE, LANES), lambda p, b, t: (b, t, c0 + p))

        def prev(col, g=g, ext=ext, nb=nb):
            c0 = (col + g * ATT_WIDTH) // LANES
            return pl.BlockSpec((1, ext, LANES), lambda p, b, t: (b, jnp.maximum(t * nb - 1, 0), c0 + p))

        in_specs += [cur(COL_Q), prev(COL_K), cur(COL_K), prev(COL_V), cur(COL_V)]
        args += [P] * 5
    bias = jnp.asarray(np.stack([_prompt_bias(g) for g in range(N_GROUPS)]))
    in_specs.append(pl.BlockSpec((N_GROUPS, 2, 2, ATT_BLOCK, 2 * ATT_BLOCK), lambda p, b, t: (0, 0, p, 0, 0)))
    return pl.pallas_call(
        _attn_prompt_kernel,
        grid=(pairs, B, T // ATT_TILE),
        in_specs=in_specs,
        out_specs=pl.BlockSpec((1, ATT_TILE, LANES), lambda p, b, t: (b, t, p)),
        out_shape=jax.ShapeDtypeStruct((B, T, ATT_WIDTH), F32),
        scratch_shapes=[pltpu.VMEM((N_GROUPS, ATT_TILE, LANES), F32),
                        pltpu.VMEM((N_GROUPS, ATT_TILE, LANES), F32)],
        compiler_params=_params(("arbitrary", "arbitrary", "arbitrary")),
        name="attn_prompt",
    )(*args, bias)


def _sample_bias(g, S):
    win, dil = ATT_GROUPS[g]
    slopes = _alibi_slopes()[g]
    s_idx = np.tile(np.arange(S), HEADS)[:, None]
    slope = np.repeat(slopes, S)[:, None]

    def bias(pos, extra_valid):
        dist = win + s_idx - pos[None, :]
        valid = (dist >= 0) & (dist <= win) & (dist % dil == 0) & extra_valid[None, :]
        return np.where(valid, -slope * dist.astype(np.float32), np.float32(NEG_INF)).astype(np.float32)

    main = bias(np.arange(win) + S, np.ones(win, bool))
    head = bias(np.arange(LANES), np.arange(LANES) < S)
    return main, head


def _attn_sample_kernel(S, cache_ref, q_ref, k_ref, v_ref, bm_ref, bh_ref, mask_ref,
                        new_ref, o_ref, lse_ref):
    nbs, rows2, w = cache_ref.shape
    nch = w // LANES
    keep = lax.broadcasted_iota(jnp.int32, (rows2, LANES), 1) < LANES - S
    mask = mask_ref[...]

    def one(bi, c):
        kv_new = jnp.concatenate([k_ref[bi], v_ref[bi]], axis=1)
        nxt_tile = jnp.concatenate([jnp.zeros((LANES - S, rows2), F32), kv_new], axis=0).T
        cur = pltpu.roll(cache_ref[bi, :, 0:LANES], LANES - S, 1)
        for j in range(nch):
            nxt = (pltpu.roll(cache_ref[bi, :, (j + 1) * LANES:(j + 2) * LANES], LANES - S, 1)
                   if j + 1 < nch else nxt_tile)
            new_ref[bi, :, j * LANES:(j + 1) * LANES] = jnp.where(keep, cur, nxt)
            cur = nxt

        q = q_ref[bi] * (HEAD_DIM ** -0.5)
        qbd = (jnp.concatenate([q] * HEADS, axis=0) * mask).astype(BF16)
        kt_main = new_ref[bi, 0:ATT_WIDTH, :].astype(BF16)
        vt_main = new_ref[bi, ATT_WIDTH:rows2, :].astype(BF16)
        kt_head = cache_ref[bi, 0:ATT_WIDTH, 0:LANES].astype(BF16)
        vt_head = cache_ref[bi, ATT_WIDTH:rows2, 0:LANES].astype(BF16)
        s_main = _dot(qbd, kt_main) + bm_ref[...]
        s_head = _dot(qbd, kt_head) + bh_ref[...]
        m = jnp.maximum(jnp.max(s_main, axis=-1, keepdims=True), jnp.max(s_head, axis=-1, keepdims=True))
        e_main = jnp.exp(s_main - m)
        e_head = jnp.exp(s_head - m)
        l = jnp.sum(e_main, axis=-1, keepdims=True) + jnp.sum(e_head, axis=-1, keepdims=True)
        o = (_dot_nt(e_main.astype(BF16), vt_main) + _dot_nt(e_head.astype(BF16), vt_head)) * (1.0 / l)
        o = o * mask
        lse = (m + jnp.log(l)) * mask
        o_acc, lse_acc = o[0:S], lse[0:S]
        for h in range(1, HEADS):
            o_acc = o_acc + o[h * S:(h + 1) * S]
            lse_acc = lse_acc + lse[h * S:(h + 1) * S]
        o_ref[bi] = o_acc
        lse_ref[bi] = lse_acc
        return c

    lax.fori_loop(0, nbs, one, 0)


def _attn_sample(Ps, cache, g, nbs):
    B, S, _ = Ps.shape
    w = cache.shape[1]
    cache_t = jnp.transpose(cache, (0, 2, 3, 4, 1)).reshape(B, 2 * ATT_WIDTH, w)
    bm, bh = _sample_bias(g, S)
    mask = (np.arange(HEADS * S)[:, None] // S == np.arange(ATT_WIDTH)[None, :] // HEAD_DIM).astype(np.float32)
    col = lambda c: pl.BlockSpec((nbs, S, ATT_WIDTH), lambda b: (b, 0, c // ATT_WIDTH + g))
    const = lambda a: pl.BlockSpec(a.shape, lambda b: (0, 0))
    out_spec = pl.BlockSpec((nbs, S, ATT_WIDTH), lambda b: (b, 0, 0))
    win_spec = pl.BlockSpec((nbs, 2 * ATT_WIDTH, w), lambda b: (b, 0, 0))
    new, o, lse = pl.pallas_call(
        functools.partial(_attn_sample_kernel, S),
        grid=(B // nbs,),
        in_specs=[win_spec, col(COL_Q), col(COL_K), col(COL_V), const(bm), const(bh), const(mask)],
        out_specs=[win_spec, out_spec, out_spec],
        out_shape=[jax.ShapeDtypeStruct((B, 2 * ATT_WIDTH, w), F32),
                   jax.ShapeDtypeStruct((B, S, ATT_WIDTH), F32),
                   jax.ShapeDtypeStruct((B, S, ATT_WIDTH), F32)],
        compiler_params=_params(("arbitrary",)),
        name=f"attn_sample_g{g}",
    )(cache_t, Ps, Ps, Ps, jnp.asarray(bm), jnp.asarray(bh), jnp.asarray(mask))
    new = jnp.transpose(new.reshape(B, 2, HEADS, HEAD_DIM, w), (0, 4, 1, 2, 3))
    return new, o, lse


def _ssd_kernel(L, xbc_ref, z_ref, dt_ref, convin_ref, statein_ref, convw_ref, convb_ref,
                dtb_ref, a_ref, dskip_ref, ng_ref, y_ref, state_ref, xpad_s, act_s, y_s, xw_s):
    LP = SSD_CHUNK
    pad = CONV_WIDTH - 1
    base = 8

    @pl.when(pl.program_id(1) == 0)
    def _():
        state_ref[...] = statein_ref[...]
        xpad_s[base - pad:base, :] = convin_ref[0]

    xpad_s[base:base + L, :] = xbc_ref[0]
    cw = 512
    for cc in range(CONV_DIM // cw):
        cl = slice(cc * cw, (cc + 1) * cw)
        conv = convb_ref[:, cl] + convw_ref[pad:pad + 1, cl] * xpad_s[base:base + L, cl]
        for i in range(pad):
            conv = conv + convw_ref[i:i + 1, cl] * xpad_s[base - pad + i:base - pad + i + L, cl]
        act_s[:, cl] = _silu(conv)
    xpad_s[base - pad:base, :] = xbc_ref[0, L - pad:L, :]

    dt = _softplus(dt_ref[0] + dtb_ref[...])
    da = _pad_rows(dt * a_ref[...], LP)
    row = lax.broadcasted_iota(jnp.int32, (LP, LP), 0)
    colm = lax.broadcasted_iota(jnp.int32, (LP, LP), 1)
    tril = (row >= colm).astype(F32)
    eye = (row == colm).astype(F32)
    cs = jnp.dot(tril, da, precision=HIGHEST, preferred_element_type=F32)
    cs_t = _dot_nt(eye, cs, precision=HIGHEST)
    cs_l = cs[0:L]
    cs_last = cs[LP - 1:LP]
    ecs = jnp.exp(cs_l)
    dte = jnp.exp(cs_last - cs_l)
    etot = jnp.exp(cs_last)
    causal = row[0:L] >= colm[0:L]
    lo = lax.broadcasted_iota(jnp.int32, (L, LANES), 1) < SSM_HEAD_DIM

    def pair(v, ha):
        return jnp.where(lo, v[:, ha:ha + 1], v[:, ha + 1:ha + 2])

    hpg = SSM_HEADS // SSM_GROUPS
    gw = hpg * SSM_HEAD_DIM
    for g in range(SSM_GROUPS):
        bg = act_s[:, D_INNER + g * D_STATE:D_INNER + (g + 1) * D_STATE]
        cg = act_s[:, D_INNER + SSM_GROUPS * D_STATE + g * D_STATE:
                   D_INNER + SSM_GROUPS * D_STATE + (g + 1) * D_STATE]
        bg_pad = _pad_rows(bg, LP).astype(BF16)
        cb = _dot_nt(cg.astype(BF16), bg_pad) if L >= 16 else _dot_nt(cg, _pad_rows(bg, LP))
        sg = state_ref[0, g * gw:(g + 1) * gw, :]
        yoff_g = _dot_nt(cg.astype(BF16), sg.astype(BF16)) if L >= 16 else _dot_nt(cg, sg)
        for hq in range(hpg // 2):
            ha = g * hpg + 2 * hq
            sl = slice(ha * SSM_HEAD_DIM, (ha + 2) * SSM_HEAD_DIM)
            xdt = act_s[:, sl] * pair(dt, ha)
            xdt_pad = _pad_rows(xdt, LP).astype(BF16)
            ys = []
            for h in (ha, ha + 1):
                seg = cs_l[:, h:h + 1] - cs_t[h:h + 1, :]
                m = cb * jnp.exp(jnp.where(causal, seg, -jnp.inf))
                ys.append(_dot(m.astype(BF16), xdt_pad) if L >= 16 else _dot(m, xdt_pad.astype(F32)))
            ydiag = jnp.where(lo, ys[0], ys[1])
            yoff = yoff_g[:, 2 * hq * SSM_HEAD_DIM:(2 * hq + 2) * SSM_HEAD_DIM] * pair(ecs, ha)
            y_s[:, sl] = ydiag + yoff
            xw_s[0:L, 2 * hq * SSM_HEAD_DIM:(2 * hq + 2) * SSM_HEAD_DIM] = xdt * pair(dte, ha)
        if L < LP:
            xw_s[L:LP, :] = jnp.zeros((LP - L, gw), F32)
        upd = _dot(xw_s[...].T.astype(BF16), bg_pad)
        for hh in range(hpg):
            h = g * hpg + hh
            rows = slice(h * SSM_HEAD_DIM, (h + 1) * SSM_HEAD_DIM)
            urows = slice(hh * SSM_HEAD_DIM, (hh + 1) * SSM_HEAD_DIM)
            state_ref[0, rows, :] = state_ref[0, rows, :] * etot[0:1, h:h + 1] + upd[urows]

    for g in range(SSM_GROUPS):
        gl = slice(g * gw, (g + 1) * gw)
        yg = y_s[:, gl] + dskip_ref[:, gl] * act_s[:, gl]
        ug = yg * _silu(z_ref[0, :, gl])
        ms = jnp.mean(ug * ug, axis=-1, keepdims=True)
        y_ref[0, :, gl] = ug * lax.rsqrt(ms + NORM_EPS) * ng_ref[:, gl]


def _ssd(P, dt, conv_in, state_in, L, conv_w, conv_b, dt_bias, a_log, d_skip, ssm_norm_g):
    B, T, _ = P.shape
    nc = T // L
    pad_lanes = lambda v: jnp.pad(v.astype(F32), (0, LANES - SSM_HEADS)).reshape(1, LANES)
    a_neg = pad_lanes(-jnp.exp(a_log.astype(F32)))
    dskip = jnp.repeat(d_skip.astype(F32), SSM_HEAD_DIM).reshape(1, D_INNER)
    const = lambda shape: pl.BlockSpec(shape, lambda b, c: (0, 0))
    y, state = pl.pallas_call(
        functools.partial(_ssd_kernel, L),
        grid=(B, nc),
        in_specs=[pl.BlockSpec((1, L, CONV_DIM), lambda b, c: (b, c, COL_XBC // CONV_DIM)),
                  pl.BlockSpec((1, L, D_INNER), lambda b, c: (b, c, COL_Z // D_INNER)),
                  pl.BlockSpec((1, L, LANES), lambda b, c: (b, c, 0)),
                  pl.BlockSpec((1, CONV_WIDTH - 1, CONV_DIM), lambda b, c: (b, 0, 0)),
                  pl.BlockSpec((1, D_INNER, D_STATE), lambda b, c: (b, 0, 0)),
                  const((CONV_WIDTH, CONV_DIM)), const((1, CONV_DIM)), const((1, LANES)),
                  const((1, LANES)), const((1, D_INNER)), const((1, D_INNER))],
        out_specs=[pl.BlockSpec((1, L, D_INNER), lambda b, c: (b, c, 0)),
                   pl.BlockSpec((1, D_INNER, D_STATE), lambda b, c: (b, 0, 0))],
        out_shape=[jax.ShapeDtypeStruct((B, T, D_INNER), F32),
                   jax.ShapeDtypeStruct((B, D_INNER, D_STATE), F32)],
        scratch_shapes=[pltpu.VMEM((8 + L, CONV_DIM), F32),
                        pltpu.VMEM((L, CONV_DIM), F32),
                        pltpu.VMEM((L, D_INNER), F32),
                        pltpu.VMEM((SSD_CHUNK, D_INNER // SSM_GROUPS), F32)],
        compiler_params=_params(("arbitrary", "arbitrary")),
        name=f"ssd_L{L}",
    )(P, P, dt, conv_in, state_in.reshape(B, D_INNER, D_STATE), conv_w, conv_b.reshape(1, -1),
      pad_lanes(dt_bias), a_neg, dskip, ssm_norm_g.reshape(1, -1))
    return y, state.reshape(state_in.shape)


def _back_kernel(n_att, *refs):
    att_refs = refs[:n_att]
    (gatt_ref, yssm_ref, ga_ref, gb_ref, x_ref, gate_ref, watt_ref, wssm_ref, wout_ref, fg_ref,
     y_ref) = refs[n_att:]
    bb, r, _ = x_ref.shape
    n = bb * r
    flat = lambda ref: ref[...].reshape(n, ref.shape[-1])
    if n_att == 1:
        att = flat(att_refs[0])
    else:
        os_, ls_ = att_refs[:N_GROUPS], att_refs[N_GROUPS:]
        l0, l1, l2 = flat(ls_[0]), flat(ls_[1]), flat(ls_[2])
        m = jnp.maximum(jnp.maximum(l0, l1), l2)
        e0, e1, e2 = jnp.exp(l0 - m), jnp.exp(l1 - m), jnp.exp(l2 - m)
        att = (e0 * flat(os_[0]) + e1 * flat(os_[1]) + e2 * flat(os_[2])) * (1.0 / (e0 + e1 + e2))
    a_out = _dot((att * _silu(flat(gatt_ref))).astype(BF16), watt_ref[...])
    m_out = _dot(flat(yssm_ref).astype(BF16), wssm_ref[...])
    merged = _sigmoid(flat(ga_ref)) * a_out + _sigmoid(flat(gb_ref)) * m_out
    res = _dot(merged.astype(BF16), wout_ref[...]).reshape(bb, r, D_MODEL)
    xo = x_ref[...] + gate_ref[...] * res
    ms = jnp.mean(xo * xo, axis=-1, keepdims=True)
    y_ref[...] = xo * lax.rsqrt(ms + NORM_EPS) * fg_ref[...]


def _back(x, mod, P, atts, y_ssm, w_att, w_ssm, w_out, final_g, bb, r):
    nb, rr, _ = x.shape
    grid = (nb // bb, rr // r)
    row = lambda width, col: pl.BlockSpec((bb, r, width), lambda b, i: (b, i, col // width))
    const = lambda a: pl.BlockSpec(a.shape, lambda b, i: (0,) * a.ndim)
    fg = final_g.reshape(1, -1)
    return pl.pallas_call(
        functools.partial(_back_kernel, len(atts)),
        grid=grid,
        in_specs=[row(ATT_WIDTH, 0)] * len(atts)
        + [row(ATT_WIDTH, COL_GATT), row(D_INNER, 0), row(D_MODEL, COL_GA), row(D_MODEL, COL_GB),
           row(D_MODEL, 0), pl.BlockSpec((bb, 1, D_MODEL), lambda b, i: (b, 0, 2)),
           const(w_att), const(w_ssm), const(w_out), const(fg)],
        out_specs=row(D_MODEL, 0),
        out_shape=jax.ShapeDtypeStruct(x.shape, F32),
        compiler_params=_params(("arbitrary", "arbitrary")),
        name="back_proj",
    )(*atts, P, y_ssm, P, P, x, mod, w_att, w_ssm, w_out, fg)


def kernel(x_prompt, x_sample, c_prompt, c_sample, cache_kv_w128, cache_kv_w512, cache_kv_w2048,
           state_ssm, state_conv, norm_g, w_ada, b_ada, w_in, conv_w, conv_b, dt_bias, a_log,
           d_skip, ssm_norm_g, w_att_branch, w_ssm_branch, w_out, final_norm_g):
    depth = w_in.shape[0]
    assert depth == 1
    B, T, _ = x_prompt.shape
    Bs, S, _ = x_sample.shape
    caches = (cache_kv_w128, cache_kv_w512, cache_kv_w2048)
    l = 0

    offs = np.cumsum((0,) + IN_SIZES)
    sec = lambda i: w_in[l][:, offs[i]:offs[i + 1]]
    w_main = jnp.concatenate([sec(5), sec(0), sec(1), sec(2), sec(3), sec(4), sec(7), sec(8)], axis=1).astype(BF16)
    w_dt = jnp.pad(sec(6), ((0, 0), (0, LANES - SSM_HEADS))).astype(BF16)
    w_att = w_att_branch[l].astype(BF16)
    w_ssm = w_ssm_branch[l].astype(BF16)
    w_o = w_out[l].astype(BF16)

    n_pad = -(B + Bs) % 8
    c_all = jnp.concatenate([c_prompt, c_sample, jnp.zeros((n_pad, D_MODEL), F32)], axis=0)
    mod = _modulation(c_all, w_ada[l], b_ada[l])
    mod_p = mod[:B].reshape(B, 1, 3 * D_MODEL)
    mod_s = mod[B:B + Bs].reshape(Bs, 1, 3 * D_MODEL)

    ssm_args = (conv_w[l], conv_b[l], dt_bias[l], a_log[l], d_skip[l], ssm_norm_g[l])

    Pp, dtp = _front(x_prompt, mod_p, norm_g[l], w_main, w_dt, bb=1, r=1024, tn=1024)
    att_p = _attn_prompt(Pp)
    kv_p = []
    for g, (win, _) in enumerate(ATT_GROUPS):
        keep = min(win, T)
        kk = Pp[:, T - keep:, COL_K + g * ATT_WIDTH:COL_K + (g + 1) * ATT_WIDTH]
        vv = Pp[:, T - keep:, COL_V + g * ATT_WIDTH:COL_V + (g + 1) * ATT_WIDTH]
        kv_p.append(jnp.stack([kk.reshape(B, keep, HEADS, HEAD_DIM), vv.reshape(B, keep, HEADS, HEAD_DIM)],
                              axis=2)[None])
    zeros_state = jnp.zeros((B, SSM_HEADS, SSM_HEAD_DIM, D_STATE), F32)
    zeros_conv = jnp.zeros((B, CONV_WIDTH - 1, CONV_DIM), F32)
    y_ssm_p, ssm_p = _ssd(Pp, dtp, zeros_conv, zeros_state, SSD_CHUNK, *ssm_args)
    conv_p = Pp[:, T - (CONV_WIDTH - 1):, COL_XBC:COL_XBC + CONV_DIM]
    y_prompt = _back(x_prompt, mod_p, Pp, [att_p], y_ssm_p, w_att, w_ssm, w_o, final_norm_g, bb=1, r=256)

    Ps, dts = _front(x_sample, mod_s, norm_g[l], w_main, w_dt, bb=32, r=S, tn=1024)
    outs, lses, kv_s = [], [], []
    for g, nbs in enumerate((8, 4, 1)):
        new, o, lse = _attn_sample(Ps, caches[g][l], g, nbs)
        outs.append(o)
        lses.append(lse)
        kv_s.append(new[None])
    y_ssm_s, ssm_s = _ssd(Ps, dts, state_conv[l], state_ssm[l], S, *ssm_args)
    conv_s = jnp.concatenate([state_conv[l], Ps[:, :, COL_XBC:COL_XBC + CONV_DIM]], axis=1)[:, -(CONV_WIDTH - 1):]
    y_sample = _back(x_sample, mod_s, Ps, outs + lses, y_ssm_s, w_att, w_ssm, w_o, final_norm_g, bb=32, r=S)

    return (y_prompt, y_sample, kv_p[0], kv_p[1], kv_p[2], ssm_p[None], conv_p[None],
            kv_s[0], kv_s[1], kv_s[2], ssm_s[None], conv_s[None])
```

```python
import functools
from typing import Callable, NamedTuple

import numpy as np
import jax
import jax.numpy as jnp
from jax import lax
from jax.experimental import pallas as pl
from jax.experimental.pallas import tpu as pltpu

F32 = jnp.float32
BF16 = jnp.bfloat16
HIGHEST = lax.Precision.HIGHEST

D_MODEL = 1024
ATT_GROUPS = ((128, 1), (512, 4), (2048, 16))
N_GROUPS = len(ATT_GROUPS)
HEAD_DIM = 64
HEADS = 8
ATT_WIDTH = HEADS * HEAD_DIM
QKV_WIDTH = N_GROUPS * ATT_WIDTH
ATT_BLOCK = 128
ALIBI_MAX_EXP = 8.0
D_INNER = 2048
SSM_HEAD_DIM = 64
SSM_HEADS = 32
SSM_GROUPS = 4
D_STATE = 128
CONV_WIDTH = 4
CONV_DIM = D_INNER + 2 * SSM_GROUPS * D_STATE
SSD_CHUNK = 128
IN_SIZES = (QKV_WIDTH, QKV_WIDTH, QKV_WIDTH, ATT_WIDTH, D_INNER, CONV_DIM, SSM_HEADS, D_MODEL, D_MODEL)
NORM_EPS = 1e-6
NEG_INF = -1e30
LOG2E = 1.4426950408889634

LANES = 128
P_WIDTH = 12288
COL_XBC = 0
COL_Q = 3072
COL_K = COL_Q + QKV_WIDTH
COL_V = COL_K + QKV_WIDTH
COL_GATT = COL_V + QKV_WIDTH
COL_Z = COL_GATT + ATT_WIDTH
COL_GA = COL_Z + D_INNER
COL_GB = COL_GA + D_MODEL
VMEM_LIMIT = 48 * 1024 * 1024
FRONT_VMEM_LIMIT = 56 * 1024 * 1024
FUSED_VMEM_LIMIT = 58 * 1024 * 1024


def _sigmoid(x):
    return 0.5 * (1.0 + jnp.tanh(0.5 * x))


def _silu(x):
    h = 0.5 * x
    return h + h * jnp.tanh(h)


def _softplus(x):
    return jnp.maximum(x, 0.0) + jnp.log(1.0 + jnp.exp(-jnp.abs(x)))


def _dot(a, b):
    return jnp.dot(a, b, preferred_element_type=F32)


def _dot_nt(a, b, precision=None):
    return lax.dot_general(a, b, (((1,), (1,)), ((), ())), precision=precision,
                           preferred_element_type=F32)


def _pad_rows(x, rows):
    if x.shape[0] == rows:
        return x
    return jnp.concatenate([x, jnp.zeros((rows - x.shape[0],) + x.shape[1:], x.dtype)], axis=0)


def _params(sem, vmem_limit=VMEM_LIMIT):
    return pltpu.CompilerParams(dimension_semantics=sem, vmem_limit_bytes=vmem_limit)


def _mod_kernel(c_ref, w_ref, b_ref, o_ref):
    s = _silu(c_ref[...]).astype(BF16)
    o_ref[...] = _dot(s, w_ref[...].astype(BF16)) + b_ref[...]


def _modulation(c, w_ada, b_ada):
    n = c.shape[0]
    tn = 512
    return pl.pallas_call(
        _mod_kernel,
        grid=(3 * D_MODEL // tn,),
        in_specs=[pl.BlockSpec((n, D_MODEL), lambda j: (0, 0)),
                  pl.BlockSpec((D_MODEL, tn), lambda j: (0, j)),
                  pl.BlockSpec((1, tn), lambda j: (0, j))],
        out_specs=pl.BlockSpec((n, tn), lambda j: (0, j)),
        out_shape=jax.ShapeDtypeStruct((n, 3 * D_MODEL), F32),
        compiler_params=_params(("arbitrary",)),
        name="adaln_mod",
    )(c, w_ada, b_ada.reshape(1, -1))


def _front_kernel(x_ref, shift_ref, scale_ref, g_ref, w_ref, wdt_ref, out_ref, dt_ref, h_ref):
    bb, r, _ = x_ref.shape

    @pl.when(pl.program_id(2) == 0)
    def _():
        x = x_ref[...]
        ms = jnp.mean(x * x, axis=-1, keepdims=True)
        xn = x * lax.rsqrt(ms + NORM_EPS) * g_ref[...]
        h = xn * (1.0 + scale_ref[...]) + shift_ref[...]
        h2 = h.reshape(bb * r, D_MODEL).astype(BF16)
        h_ref[...] = h2
        dt_ref[...] = _dot(h2, wdt_ref[...]).reshape(bb, r, LANES)

    tn = out_ref.shape[-1]
    cols = pl.ds(pl.multiple_of(pl.program_id(2) * tn, tn), tn)
    out_ref[...] = _dot(h_ref[...], w_ref[:, cols]).reshape(out_ref.shape)


def _front(x, mod, norm_g, w_main, w_dt, bb, r, tn):
    nb, rr, _ = x.shape
    grid = (nb // bb, rr // r, P_WIDTH // tn)
    return pl.pallas_call(
        _front_kernel,
        grid=grid,
        in_specs=[pl.BlockSpec((bb, r, D_MODEL), lambda b, i, j: (b, i, 0)),
                  pl.BlockSpec((bb, 1, D_MODEL), lambda b, i, j: (b, 0, 0)),
                  pl.BlockSpec((bb, 1, D_MODEL), lambda b, i, j: (b, 0, 1)),
                  pl.BlockSpec((1, D_MODEL), lambda b, i, j: (0, 0)),
                  pl.BlockSpec((D_MODEL, P_WIDTH), lambda b, i, j: (0, 0), pipeline_mode=pl.Buffered(1)),
                  pl.BlockSpec((D_MODEL, LANES), lambda b, i, j: (0, 0))],
        out_specs=[pl.BlockSpec((bb, r, tn), lambda b, i, j: (b, i, j)),
                   pl.BlockSpec((bb, r, LANES), lambda b, i, j: (b, i, 0))],
        out_shape=[jax.ShapeDtypeStruct((nb, rr, P_WIDTH), F32),
                   jax.ShapeDtypeStruct((nb, rr, LANES), F32)],
        scratch_shapes=[pltpu.VMEM((bb * r, D_MODEL), BF16)],
        compiler_params=_params(("arbitrary", "arbitrary", "arbitrary"), FRONT_VMEM_LIMIT),
        name="front_proj",
    )(x, mod, mod, norm_g.reshape(1, -1), w_main, w_dt)


def _alibi_slopes():
    n = N_GROUPS * HEADS
    m = 2.0 ** (-ALIBI_MAX_EXP * np.arange(1, n + 1) / n)
    return m.reshape(N_GROUPS, HEADS).astype(np.float32)


def _prompt_bias(g):
    win, dil = ATT_GROUPS[g]
    span = win // dil
    qi = np.arange(ATT_BLOCK)[:, None]
    kj = np.arange(2 * ATT_BLOCK)[None, :]
    delta = qi + ATT_BLOCK - kj
    valid = (delta >= 0) & (delta <= span)
    slopes = _alibi_slopes()[g]
    alibi = -slopes[:, None, None] * (delta * dil).astype(np.float32)[None]
    later = np.where(valid[None], alibi, np.float32(NEG_INF))
    first = np.where((valid & (kj >= ATT_BLOCK))[None], alibi, np.float32(NEG_INF))
    return np.stack([first, later]).astype(np.float32)


ATT_TILE = ATT_BLOCK * max(d for _, d in ATT_GROUPS)
ATT_UNROLL = 8


def _attn_pair(q, k, v, bias, lo):
    outs, lses = [], []
    for half in range(2):
        qm = jnp.where(lo if half == 0 else jnp.logical_not(lo), q, 0.0).astype(BF16)
        s = _dot_nt(qm, k) + bias(half)
        m = jnp.max(s, axis=-1, keepdims=True)
        e = jnp.exp(s - m)
        l = jnp.sum(e, axis=-1, keepdims=True)
        outs.append(_dot(e.astype(BF16), v) * (1.0 / l))
        lses.append(m + jnp.log(l))
    return jnp.where(lo, outs[0], outs[1]), jnp.where(lo, lses[0], lses[1])


def _attn_prompt_kernel(*refs):
    ins, (bias_ref, att_ref, o_s, lse_s) = refs[:5 * N_GROUPS], refs[5 * N_GROUPS:]
    lo = lax.broadcasted_iota(jnp.int32, (ATT_BLOCK, LANES), 1) < HEAD_DIM
    scale = HEAD_DIM ** -0.5
    lead_var = jnp.minimum(pl.program_id(2), 1)

    for g, (_, d) in enumerate(ATT_GROUPS):
        q_ref, kp_ref, kc_ref, vp_ref, vc_ref = ins[5 * g:5 * g + 5]
        ext = ATT_BLOCK * d
        nb = ATT_TILE // ext

        def rows(start, n, d=d):
            return pl.ds(start, n) if d == 1 else pl.ds(start, n, stride=d)

        def emit(start, o, lse, g=g, rows=rows):
            o_s[g, rows(start, ATT_BLOCK), :] = o
            lse_s[g, rows(start, ATT_BLOCK), :] = lse

        def lead(r, c, g=g, rows=rows, emit=emit, q_ref=q_ref, kp_ref=kp_ref, kc_ref=kc_ref,
                 vp_ref=vp_ref, vc_ref=vc_ref):
            sel = rows(r, ATT_BLOCK)
            k = jnp.concatenate([kp_ref[0, sel, :], kc_ref[0, sel, :]], axis=0).astype(BF16)
            v = jnp.concatenate([vp_ref[0, sel, :], vc_ref[0, sel, :]], axis=0).astype(BF16)
            o, lse = _attn_pair(q_ref[0, sel, :] * scale, k, v,
                                lambda half: bias_ref[g, lead_var, half], lo)
            emit(r, o, lse)
            return c

        def later(n, c, g=g, rows=rows, emit=emit, ext=ext, nb=nb, q_ref=q_ref, kc_ref=kc_ref, vc_ref=vc_ref):
            if d == 1:
                prev = pl.multiple_of(ext * n, ATT_BLOCK)
            else:
                r = n // (nb - 1)
                prev = r + ext * (n - r * (nb - 1))
            k = kc_ref[0, rows(prev, 2 * ATT_BLOCK), :].astype(BF16)
            v = vc_ref[0, rows(prev, 2 * ATT_BLOCK), :].astype(BF16)
            o, lse = _attn_pair(q_ref[0, rows(prev + ext, ATT_BLOCK), :] * scale, k, v,
                                lambda half: bias_ref[g, 1, half], lo)
            emit(prev + ext, o, lse)
            return c

        lax.fori_loop(0, d, lead, 0, unroll=min(d, ATT_UNROLL))
        if nb > 1:
            trips = d * (nb - 1)
            lax.fori_loop(0, trips, later, 0, unroll=max(u for u in range(1, ATT_UNROLL + 2) if trips % u == 0))

    cr = 256

    def combine(i, c):
        sel = pl.ds(pl.multiple_of(i * cr, cr), cr)
        l0, l1, l2 = lse_s[0, sel, :], lse_s[1, sel, :], lse_s[2, sel, :]
        m = jnp.maximum(jnp.maximum(l0, l1), l2)
        e0, e1, e2 = jnp.exp(l0 - m), jnp.exp(l1 - m), jnp.exp(l2 - m)
        att_ref[0, sel, :] = ((e0 * o_s[0, sel, :] + e1 * o_s[1, sel, :] + e2 * o_s[2, sel, :])
                              * (1.0 / (e0 + e1 + e2)))
        return c

    lax.fori_loop(0, ATT_TILE // cr, combine, 0)


def _attn_prompt(P):
    B, T, _ = P.shape
    assert T % ATT_TILE == 0
    pairs = ATT_WIDTH // LANES
    in_specs, args = [], []
    for g, (_, d) in enumerate(ATT_GROUPS):
        ext = ATT_BLOCK * d
        nb = ATT_TILE // ext

        def cur(col, g=g):
            c0 = (col + g * ATT_WIDTH) // LANES
            return pl.BlockSpec((1, ATT_TILE, LANES), lambda p, b, t: (b, t, c0 + p))

        def prev(col, g=g, ext=ext, nb=nb):
            c0 = (col + g * ATT_WIDTH) // LANES
            return pl.BlockSpec((1, ext, LANES), lambda p, b, t: (b, jnp.maximum(t * nb - 1, 0), c0 + p))

        in_specs += [cur(COL_Q), prev(COL_K), cur(COL_K), prev(COL_V), cur(COL_V)]
        args += [P] * 5
    bias = jnp.asarray(np.stack([_prompt_bias(g) for g in range(N_GROUPS)]))
    in_specs.append(pl.BlockSpec((N_GROUPS, 2, 2, ATT_BLOCK, 2 * ATT_BLOCK), lambda p, b, t: (0, 0, p, 0, 0)))
    return pl.pallas_call(
        _attn_prompt_kernel,
        grid=(pairs, B, T // ATT_TILE),
        in_specs=in_specs,
        out_specs=pl.BlockSpec((1, ATT_TILE, LANES), lambda p, b, t: (b, t, p)),
        out_shape=jax.ShapeDtypeStruct((B, T, ATT_WIDTH), F32),
        scratch_shapes=[pltpu.VMEM((N_GROUPS, ATT_TILE, LANES), F32),
                        pltpu.VMEM((N_GROUPS, ATT_TILE, LANES), F32)],
        compiler_params=_params(("arbitrary", "arbitrary", "arbitrary")),
        name="attn_prompt",
    )(*args, bias)


def _sample_bias(g, S):
    win, dil = ATT_GROUPS[g]
    slopes = _alibi_slopes()[g]
    s_idx = np.tile(np.arange(S), HEADS)[:, None]
    slope = np.repeat(slopes, S)[:, None]

    def bias(pos, extra_valid):
        dist = win + s_idx - pos[None, :]
        valid = (dist >= 0) & (dist <= win) & (dist % dil == 0) & extra_valid[None, :]
        return np.where(valid, -slope * dist.astype(np.float32), np.float32(NEG_INF)).astype(np.float32)

    main = bias(np.arange(win) + S, np.ones(win, bool))
    head = bias(np.arange(LANES), np.arange(LANES) < S)
    return main, head


def _attn_sample_kernel(S, cache_ref, q_ref, k_ref, v_ref, bm_ref, bh_ref, mask_ref,
                        new_ref, o_ref, lse_ref):
    nbs, rows2, w = cache_ref.shape
    nch = w // LANES
    keep = lax.broadcasted_iota(jnp.int32, (rows2, LANES), 1) < LANES - S
    mask = mask_ref[...]

    def one(bi, c):
        kv_new = jnp.concatenate([k_ref[bi], v_ref[bi]], axis=1)
        nxt_tile = jnp.concatenate([jnp.zeros((LANES - S, rows2), F32), kv_new], axis=0).T
        cur = pltpu.roll(cache_ref[bi, :, 0:LANES], LANES - S, 1)
        for j in range(nch):
            nxt = (pltpu.roll(cache_ref[bi, :, (j + 1) * LANES:(j + 2) * LANES], LANES - S, 1)
                   if j + 1 < nch else nxt_tile)
            new_ref[bi, :, j * LANES:(j + 1) * LANES] = jnp.where(keep, cur, nxt)
            cur = nxt

        q = q_ref[bi] * (HEAD_DIM ** -0.5)
        qbd = (jnp.concatenate([q] * HEADS, axis=0) * mask).astype(BF16)
        kt_main = new_ref[bi, 0:ATT_WIDTH, :].astype(BF16)
        vt_main = new_ref[bi, ATT_WIDTH:rows2, :].astype(BF16)
        kt_head = cache_ref[bi, 0:ATT_WIDTH, 0:LANES].astype(BF16)
        vt_head = cache_ref[bi, ATT_WIDTH:rows2, 0:LANES].astype(BF16)
        s_main = _dot(qbd, kt_main) + bm_ref[...]
        s_head = _dot(qbd, kt_head) + bh_ref[...]
        m = jnp.maximum(jnp.max(s_main, axis=-1, keepdims=True), jnp.max(s_head, axis=-1, keepdims=True))
        e_main = jnp.exp(s_main - m)
        e_head = jnp.exp(s_head - m)
        l = jnp.sum(e_main, axis=-1, keepdims=True) + jnp.sum(e_head, axis=-1, keepdims=True)
        o = (_dot_nt(e_main.astype(BF16), vt_main) + _dot_nt(e_head.astype(BF16), vt_head)) * (1.0 / l)
        o = o * mask
        lse = (m + jnp.log(l)) * mask
        o_acc, lse_acc = o[0:S], lse[0:S]
        for h in range(1, HEADS):
            o_acc = o_acc + o[h * S:(h + 1) * S]
            lse_acc = lse_acc + lse[h * S:(h + 1) * S]
        o_ref[bi] = o_acc
        lse_ref[bi] = lse_acc
        return c

    lax.fori_loop(0, nbs, one, 0, unroll=min(nbs, 2))


class _Part(NamedTuple):
    kernel: Callable
    in_specs: list
    args: list
    out_specs: list
    out_shapes: list
    scratch: list


def _run_parts(parts, grid, name, vmem_limit):
    n_in = [len(p.in_specs) for p in parts]
    n_out = [len(p.out_specs) for p in parts]
    n_scr = [len(p.scratch) for p in parts]

    def body(*refs):
        ins, outs, scr = refs[:sum(n_in)], refs[sum(n_in):sum(n_in) + sum(n_out)], refs[sum(n_in) + sum(n_out):]
        for k, p in enumerate(parts):
            take = lambda seq, counts: seq[sum(counts[:k]):sum(counts[:k + 1])]
            p.kernel(*take(ins, n_in), *take(outs, n_out), *take(scr, n_scr))

    flat = lambda field: [x for p in parts for x in getattr(p, field)]
    res = pl.pallas_call(
        body,
        grid=grid,
        in_specs=flat("in_specs"),
        out_specs=flat("out_specs"),
        out_shape=flat("out_shapes"),
        scratch_shapes=flat("scratch"),
        compiler_params=_params(("arbitrary",) * len(grid), vmem_limit),
        name=name,
    )(*flat("args"))
    return [res[sum(n_out[:k]):sum(n_out[:k + 1])] for k in range(len(parts))]


def _attn_sample_part(Ps, cache, g, batch_of):
    B, S, _ = Ps.shape
    w = cache.shape[1]
    cache_t = jnp.transpose(cache, (0, 2, 3, 4, 1)).reshape(B, 2 * ATT_WIDTH, w)
    bm, bh = _sample_bias(g, S)
    mask = (np.arange(HEADS * S)[:, None] // S == np.arange(ATT_WIDTH)[None, :] // HEAD_DIM).astype(np.float32)
    col = lambda c: pl.BlockSpec((1, S, ATT_WIDTH), lambda *ids: (batch_of(*ids), 0, c // ATT_WIDTH + g))
    const = lambda a: pl.BlockSpec(a.shape, lambda *ids: (0, 0))
    out_spec = pl.BlockSpec((1, S, ATT_WIDTH), lambda *ids: (batch_of(*ids), 0, 0))
    win_spec = pl.BlockSpec((1, 2 * ATT_WIDTH, w), lambda *ids: (batch_of(*ids), 0, 0))
    return _Part(
        kernel=functools.partial(_attn_sample_kernel, S),
        in_specs=[win_spec, col(COL_Q), col(COL_K), col(COL_V), const(bm), const(bh), const(mask)],
        args=[cache_t, Ps, Ps, Ps, jnp.asarray(bm), jnp.asarray(bh), jnp.asarray(mask)],
        out_specs=[win_spec, out_spec, out_spec],
        out_shapes=[jax.ShapeDtypeStruct((B, 2 * ATT_WIDTH, w), F32),
                    jax.ShapeDtypeStruct((B, S, ATT_WIDTH), F32),
                    jax.ShapeDtypeStruct((B, S, ATT_WIDTH), F32)],
        scratch=[])


def _window_from_position_minor(new, w):
    B = new.shape[0]
    return jnp.transpose(new.reshape(B, 2, HEADS, HEAD_DIM, w), (0, 4, 1, 2, 3))


def _ssd_kernel(L, xbc_ref, z_ref, dt_ref, convin_ref, statein_ref, convw_ref, convb_ref,
                dtb_ref, alog_ref, dskip_ref, ng_ref, y_ref, state_ref, xpad_s, act_s, y_s, xw_s):
    LP = SSD_CHUNK
    pad = CONV_WIDTH - 1
    base = 8

    @pl.when(pl.program_id(1) == 0)
    def _():
        state_ref[...] = statein_ref[...]
        xpad_s[0:base, :] = jnp.zeros((base, CONV_DIM), F32)
        xpad_s[base - pad:base, :] = convin_ref[0]

    xpad_s[base:base + L, :] = xbc_ref[0]
    cw = 512
    for cc in range(CONV_DIM // cw):
        cl = slice(cc * cw, (cc + 1) * cw)
        xfull = xpad_s[:, cl]
        x1 = pltpu.roll(xfull, 1, 0)
        u2 = pltpu.roll(convw_ref[1:2, cl] * xfull + convw_ref[0:1, cl] * x1, 2, 0)
        conv = convb_ref[:, cl] + convw_ref[3:4, cl] * xfull + convw_ref[2:3, cl] * x1 + u2
        act_s[:, cl] = _silu(conv[base:base + L])
    xpad_s[base - pad:base, :] = xbc_ref[0, L - pad:L, :]

    dt = _softplus(dt_ref[0] + dtb_ref[...])
    row = lax.broadcasted_iota(jnp.int32, (LP, LP), 0)
    colm = lax.broadcasted_iota(jnp.int32, (LP, LP), 1)
    a = jnp.where(colm[0:1] < SSM_HEADS, -jnp.exp(alog_ref[...]), 0.0)
    da = _pad_rows(dt * a, LP)
    tril = (row >= colm).astype(F32)
    eye = (row == colm).astype(F32)
    cs = jnp.dot(tril, da, precision=HIGHEST, preferred_element_type=F32)
    cs2 = cs * LOG2E
    cs2_t = _dot_nt(eye, cs2, precision=HIGHEST)
    cs_l = cs2[0:L]
    cs_last = cs2[LP - 1:LP]
    ecs = jnp.exp2(cs_l)
    dte = jnp.exp2(cs_last - cs_l)
    etot = jnp.exp2(cs_last)
    causal = row[0:L] >= colm[0:L]
    lo = lax.broadcasted_iota(jnp.int32, (L, LANES), 1) < SSM_HEAD_DIM

    def pair(v, ha):
        return jnp.where(lo, v[:, ha:ha + 1], v[:, ha + 1:ha + 2])

    hpg = SSM_HEADS // SSM_GROUPS
    gw = hpg * SSM_HEAD_DIM
    for g in range(SSM_GROUPS):
        bg = act_s[:, D_INNER + g * D_STATE:D_INNER + (g + 1) * D_STATE]
        cg = act_s[:, D_INNER + SSM_GROUPS * D_STATE + g * D_STATE:
                   D_INNER + SSM_GROUPS * D_STATE + (g + 1) * D_STATE]
        bg_pad = _pad_rows(bg, LP).astype(BF16)
        cb = _dot_nt(cg.astype(BF16), bg_pad) if L >= 16 else _dot_nt(cg, _pad_rows(bg, LP))
        sg = state_ref[0, g * gw:(g + 1) * gw, :]
        yoff_g = _dot_nt(cg.astype(BF16), sg.astype(BF16)) if L >= 16 else _dot_nt(cg, sg)
        for hq in range(hpg // 2):
            ha = g * hpg + 2 * hq
            sl = slice(ha * SSM_HEAD_DIM, (ha + 2) * SSM_HEAD_DIM)
            xdt = act_s[:, sl] * pair(dt, ha)
            xdt_pad = _pad_rows(xdt, LP).astype(BF16)
            ys = []
            for h in (ha, ha + 1):
                seg = cs_l[:, h:h + 1] - cs2_t[h:h + 1, :]
                m = cb * jnp.exp2(jnp.where(causal, seg, -jnp.inf))
                ys.append(_dot(m.astype(BF16), xdt_pad) if L >= 16 else _dot(m, xdt_pad.astype(F32)))
            ydiag = jnp.where(lo, ys[0], ys[1])
            yoff = yoff_g[:, 2 * hq * SSM_HEAD_DIM:(2 * hq + 2) * SSM_HEAD_DIM] * pair(ecs, ha)
            y_s[:, sl] = ydiag + yoff
            xw_s[0:L, 2 * hq * SSM_HEAD_DIM:(2 * hq + 2) * SSM_HEAD_DIM] = xdt * pair(dte, ha)
        if L < LP:
            xw_s[L:LP, :] = jnp.zeros((LP - L, gw), F32)
        upd = _dot(xw_s[...].T.astype(BF16), bg_pad)
        for hh in range(hpg):
            h = g * hpg + hh
            rows = slice(h * SSM_HEAD_DIM, (h + 1) * SSM_HEAD_DIM)
            urows = slice(hh * SSM_HEAD_DIM, (hh + 1) * SSM_HEAD_DIM)
            state_ref[0, rows, :] = state_ref[0, rows, :] * etot[0:1, h:h + 1] + upd[urows]

    for g in range(SSM_GROUPS):
        gl = slice(g * gw, (g + 1) * gw)
        yg = y_s[:, gl] + dskip_ref[:, gl] * act_s[:, gl]
        ug = yg * _silu(z_ref[0, :, gl])
        ms = jnp.mean(ug * ug, axis=-1, keepdims=True)
        y_ref[0, :, gl] = ug * lax.rsqrt(ms + NORM_EPS) * ng_ref[:, gl]


def _ssd_part(P, dt, conv_in, state_in, L, conv_w, conv_b, dt_bias, a_log, d_skip, ssm_norm_g):
    B, T, _ = P.shape
    pad_lanes = lambda v: jnp.pad(v.astype(F32), (0, LANES - SSM_HEADS)).reshape(1, LANES)
    dskip = jnp.repeat(d_skip.astype(F32), SSM_HEAD_DIM).reshape(1, D_INNER)
    const = lambda shape: pl.BlockSpec(shape, lambda b, c: (0, 0))
    return _Part(
        kernel=functools.partial(_ssd_kernel, L),
        in_specs=[pl.BlockSpec((1, L, CONV_DIM), lambda b, c: (b, c, COL_XBC // CONV_DIM)),
                  pl.BlockSpec((1, L, D_INNER), lambda b, c: (b, c, COL_Z // D_INNER)),
                  pl.BlockSpec((1, L, LANES), lambda b, c: (b, c, 0)),
                  pl.BlockSpec((1, CONV_WIDTH - 1, CONV_DIM), lambda b, c: (b, 0, 0)),
                  pl.BlockSpec((1, D_INNER, D_STATE), lambda b, c: (b, 0, 0)),
                  const((CONV_WIDTH, CONV_DIM)), const((1, CONV_DIM)), const((1, LANES)),
                  const((1, LANES)), const((1, D_INNER)), const((1, D_INNER))],
        out_specs=[pl.BlockSpec((1, L, D_INNER), lambda b, c: (b, c, 0)),
                   pl.BlockSpec((1, D_INNER, D_STATE), lambda b, c: (b, 0, 0))],
        out_shapes=[jax.ShapeDtypeStruct((B, T, D_INNER), F32),
                    jax.ShapeDtypeStruct((B, D_INNER, D_STATE), F32)],
        scratch=[pltpu.VMEM((8 + L, CONV_DIM), F32),
                 pltpu.VMEM((L, CONV_DIM), F32),
                 pltpu.VMEM((L, D_INNER), F32),
                 pltpu.VMEM((SSD_CHUNK, D_INNER // SSM_GROUPS), F32)],
        args=[P, P, dt, conv_in, state_in.reshape(B, D_INNER, D_STATE), conv_w, conv_b.reshape(1, -1),
              pad_lanes(dt_bias), pad_lanes(a_log), dskip, ssm_norm_g.reshape(1, -1)])


def _back_kernel(n_att, *refs):
    att_refs = refs[:n_att]
    (gatt_ref, yssm_ref, ga_ref, gb_ref, x_ref, gate_ref, watt_ref, wssm_ref, wout_ref, fg_ref,
     y_ref) = refs[n_att:]
    bb, r, _ = x_ref.shape
    n = bb * r
    flat = lambda ref: ref[...].reshape(n, ref.shape[-1])
    if n_att == 1:
        att = flat(att_refs[0])
    else:
        os_, ls_ = att_refs[:N_GROUPS], att_refs[N_GROUPS:]
        l0, l1, l2 = flat(ls_[0]), flat(ls_[1]), flat(ls_[2])
        m = jnp.maximum(jnp.maximum(l0, l1), l2)
        e0, e1, e2 = jnp.exp(l0 - m), jnp.exp(l1 - m), jnp.exp(l2 - m)
        att = (e0 * flat(os_[0]) + e1 * flat(os_[1]) + e2 * flat(os_[2])) * (1.0 / (e0 + e1 + e2))
    a_out = _dot((att * _silu(flat(gatt_ref))).astype(BF16), watt_ref[...])
    m_out = _dot(flat(yssm_ref).astype(BF16), wssm_ref[...])
    merged = _sigmoid(flat(ga_ref)) * a_out + _sigmoid(flat(gb_ref)) * m_out
    res = _dot(merged.astype(BF16), wout_ref[...]).reshape(bb, r, D_MODEL)
    xo = x_ref[...] + gate_ref[...] * res
    ms = jnp.mean(xo * xo, axis=-1, keepdims=True)
    y_ref[...] = xo * lax.rsqrt(ms + NORM_EPS) * fg_ref[...]


def _back(x, mod, P, atts, y_ssm, w_att, w_ssm, w_out, final_g, bb, r):
    nb, rr, _ = x.shape
    grid = (nb // bb, rr // r)
    row = lambda width, col: pl.BlockSpec((bb, r, width), lambda b, i: (b, i, col // width))
    const = lambda a: pl.BlockSpec(a.shape, lambda b, i: (0,) * a.ndim)
    fg = final_g.reshape(1, -1)
    return pl.pallas_call(
        functools.partial(_back_kernel, len(atts)),
        grid=grid,
        in_specs=[row(ATT_WIDTH, 0)] * len(atts)
        + [row(ATT_WIDTH, COL_GATT), row(D_INNER, 0), row(D_MODEL, COL_GA), row(D_MODEL, COL_GB),
           row(D_MODEL, 0), pl.BlockSpec((bb, 1, D_MODEL), lambda b, i: (b, 0, 2)),
           const(w_att), const(w_ssm), const(w_out), const(fg)],
        out_specs=row(D_MODEL, 0),
        out_shape=jax.ShapeDtypeStruct(x.shape, F32),
        compiler_params=_params(("arbitrary", "arbitrary")),
        name="back_proj",
    )(*atts, P, y_ssm, P, P, x, mod, w_att, w_ssm, w_out, fg)


def kernel(x_prompt, x_sample, c_prompt, c_sample, cache_kv_w128, cache_kv_w512, cache_kv_w2048,
           state_ssm, state_conv, norm_g, w_ada, b_ada, w_in, conv_w, conv_b, dt_bias, a_log,
           d_skip, ssm_norm_g, w_att_branch, w_ssm_branch, w_out, final_norm_g):
    depth = w_in.shape[0]
    assert depth == 1
    B, T, _ = x_prompt.shape
    Bs, S, _ = x_sample.shape
    caches = (cache_kv_w128, cache_kv_w512, cache_kv_w2048)
    l = 0

    offs = np.cumsum((0,) + IN_SIZES)
    sec = lambda i: w_in[l][:, offs[i]:offs[i + 1]]
    w_main = jnp.concatenate([sec(5), sec(0), sec(1), sec(2), sec(3), sec(4), sec(7), sec(8)], axis=1).astype(BF16)
    w_dt = jnp.pad(sec(6), ((0, 0), (0, LANES - SSM_HEADS))).astype(BF16)
    w_att = w_att_branch[l].astype(BF16)
    w_ssm = w_ssm_branch[l].astype(BF16)
    w_o = w_out[l].astype(BF16)

    n_pad = -(B + Bs) % 8
    c_all = jnp.concatenate([c_prompt, c_sample, jnp.zeros((n_pad, D_MODEL), F32)], axis=0)
    mod = _modulation(c_all, w_ada[l], b_ada[l])
    mod_p = mod[:B].reshape(B, 1, 3 * D_MODEL)
    mod_s = mod[B:B + Bs].reshape(Bs, 1, 3 * D_MODEL)

    ssm_args = (conv_w[l], conv_b[l], dt_bias[l], a_log[l], d_skip[l], ssm_norm_g[l])

    Pp, dtp = _front(x_prompt, mod_p, norm_g[l], w_main, w_dt, bb=1, r=1024, tn=1024)
    Ps, dts = _front(x_sample, mod_s, norm_g[l], w_main, w_dt, bb=32, r=S, tn=1024)

    zeros_state = jnp.zeros((B, SSM_HEADS, SSM_HEAD_DIM, D_STATE), F32)
    zeros_conv = jnp.zeros((B, CONV_WIDTH - 1, CONV_DIM), F32)
    nc = T // SSD_CHUNK
    assert Bs == B * nc
    (y_ssm_p, ssm_p), (new2, o2, lse2) = _run_parts(
        [_ssd_part(Pp, dtp, zeros_conv, zeros_state, SSD_CHUNK, *ssm_args),
         _attn_sample_part(Ps, caches[2][l], 2, lambda b, c: b * nc + c)],
        grid=(B, nc), name="ssd_prompt_window2", vmem_limit=FUSED_VMEM_LIMIT)
    (y_ssm_s, ssm_s), (new0, o0, lse0), (new1, o1, lse1) = _run_parts(
        [_ssd_part(Ps, dts, state_conv[l], state_ssm[l], S, *ssm_args),
         _attn_sample_part(Ps, caches[0][l], 0, lambda b, c: b),
         _attn_sample_part(Ps, caches[1][l], 1, lambda b, c: b)],
        grid=(Bs, 1), name="ssd_sample_window01", vmem_limit=VMEM_LIMIT)
    ssm_p = ssm_p.reshape(zeros_state.shape)
    ssm_s = ssm_s.reshape(state_ssm[l].shape)
    kv_s = [_window_from_position_minor(new, new.shape[-1])[None] for new in (new0, new1, new2)]

    att_p = _attn_prompt(Pp)
    kv_p = []
    for g, (win, _) in enumerate(ATT_GROUPS):
        keep = min(win, T)
        kk = Pp[:, T - keep:, COL_K + g * ATT_WIDTH:COL_K + (g + 1) * ATT_WIDTH]
        vv = Pp[:, T - keep:, COL_V + g * ATT_WIDTH:COL_V + (g + 1) * ATT_WIDTH]
        kv_p.append(jnp.stack([kk.reshape(B, keep, HEADS, HEAD_DIM), vv.reshape(B, keep, HEADS, HEAD_DIM)],
                              axis=2)[None])
    conv_p = Pp[:, T - (CONV_WIDTH - 1):, COL_XBC:COL_XBC + CONV_DIM]
    y_prompt = _back(x_prompt, mod_p, Pp, [att_p], y_ssm_p, w_att, w_ssm, w_o, final_norm_g, bb=1, r=256)

    conv_s = jnp.concatenate([state_conv[l], Ps[:, :, COL_XBC:COL_XBC + CONV_DIM]], axis=1)[:, -(CONV_WIDTH - 1):]
    y_sample = _back(x_sample, mod_s, Ps, [o0, o1, o2, lse0, lse1, lse2], y_ssm_s, w_att, w_ssm, w_o,
                     final_norm_g, bb=32, r=S)

    return (y_prompt, y_sample, kv_p[0], kv_p[1], kv_p[2], ssm_p[None], conv_p[None],
            kv_s[0], kv_s[1], kv_s[2], ssm_s[None], conv_s[None])
```

```python
import functools
from typing import Callable, NamedTuple

import numpy as np
import jax
import jax.numpy as jnp
from jax import lax
from jax.experimental import pallas as pl
from jax.experimental.pallas import tpu as pltpu

F32 = jnp.float32
BF16 = jnp.bfloat16
HIGHEST = lax.Precision.HIGHEST

D_MODEL = 1024
ATT_GROUPS = ((128, 1), (512, 4), (2048, 16))
N_GROUPS = len(ATT_GROUPS)
HEAD_DIM = 64
HEADS = 8
ATT_WIDTH = HEADS * HEAD_DIM
QKV_WIDTH = N_GROUPS * ATT_WIDTH
ATT_BLOCK = 128
ALIBI_MAX_EXP = 8.0
D_INNER = 2048
SSM_HEAD_DIM = 64
SSM_HEADS = 32
SSM_GROUPS = 4
D_STATE = 128
CONV_WIDTH = 4
CONV_DIM = D_INNER + 2 * SSM_GROUPS * D_STATE
SSD_CHUNK = 128
IN_SIZES = (QKV_WIDTH, QKV_WIDTH, QKV_WIDTH, ATT_WIDTH, D_INNER, CONV_DIM, SSM_HEADS, D_MODEL, D_MODEL)
NORM_EPS = 1e-6
NEG_INF = -1e30
LOG2E = 1.4426950408889634

LANES = 128
P_WIDTH = 12288
COL_XBC = 0
COL_Q = 3072
COL_K = COL_Q + QKV_WIDTH
COL_V = COL_K + QKV_WIDTH
COL_GATT = COL_V + QKV_WIDTH
COL_Z = COL_GATT + ATT_WIDTH
COL_GA = COL_Z + D_INNER
COL_GB = COL_GA + D_MODEL
VMEM_LIMIT = 48 * 1024 * 1024
FRONT_VMEM_LIMIT = 56 * 1024 * 1024
FUSED_VMEM_LIMIT = 58 * 1024 * 1024


def _sigmoid(x):
    return 0.5 * (1.0 + jnp.tanh(0.5 * x))


def _silu(x):
    h = 0.5 * x
    return h + h * jnp.tanh(h)


def _softplus(x):
    return jnp.maximum(x, 0.0) + jnp.log(1.0 + jnp.exp(-jnp.abs(x)))


def _dot(a, b):
    return jnp.dot(a, b, preferred_element_type=F32)


def _dot_nt(a, b, precision=None):
    return lax.dot_general(a, b, (((1,), (1,)), ((), ())), precision=precision,
                           preferred_element_type=F32)


def _pad_rows(x, rows):
    if x.shape[0] == rows:
        return x
    return jnp.concatenate([x, jnp.zeros((rows - x.shape[0],) + x.shape[1:], x.dtype)], axis=0)


def _params(sem, vmem_limit=VMEM_LIMIT):
    return pltpu.CompilerParams(dimension_semantics=sem, vmem_limit_bytes=vmem_limit)


def _mod_kernel(c_ref, w_ref, b_ref, o_ref):
    s = _silu(c_ref[...]).astype(BF16)
    o_ref[...] = _dot(s, w_ref[...].astype(BF16)) + b_ref[...]


def _modulation(c, w_ada, b_ada):
    n = c.shape[0]
    tn = 512
    return pl.pallas_call(
        _mod_kernel,
        grid=(3 * D_MODEL // tn,),
        in_specs=[pl.BlockSpec((n, D_MODEL), lambda j: (0, 0)),
                  pl.BlockSpec((D_MODEL, tn), lambda j: (0, j)),
                  pl.BlockSpec((1, tn), lambda j: (0, j))],
        out_specs=pl.BlockSpec((n, tn), lambda j: (0, j)),
        out_shape=jax.ShapeDtypeStruct((n, 3 * D_MODEL), F32),
        compiler_params=_params(("arbitrary",)),
        name="adaln_mod",
    )(c, w_ada, b_ada.reshape(1, -1))


def _front_kernel(x_ref, shift_ref, scale_ref, g_ref, w_ref, wdt_ref, out_ref, dt_ref, h_ref):
    bb, r, _ = x_ref.shape

    @pl.when(pl.program_id(2) == 0)
    def _():
        x = x_ref[...]
        ms = jnp.mean(x * x, axis=-1, keepdims=True)
        xn = x * lax.rsqrt(ms + NORM_EPS) * g_ref[...]
        h = xn * (1.0 + scale_ref[...]) + shift_ref[...]
        h2 = h.reshape(bb * r, D_MODEL).astype(BF16)
        h_ref[...] = h2
        dt_ref[...] = _dot(h2, wdt_ref[...]).reshape(bb, r, LANES)

    tn = out_ref.shape[-1]
    cols = pl.ds(pl.multiple_of(pl.program_id(2) * tn, tn), tn)
    out_ref[...] = _dot(h_ref[...], w_ref[:, cols]).reshape(out_ref.shape)


def _front(x, mod, norm_g, w_main, w_dt, bb, r, tn):
    nb, rr, _ = x.shape
    grid = (nb // bb, rr // r, P_WIDTH // tn)
    return pl.pallas_call(
        _front_kernel,
        grid=grid,
        in_specs=[pl.BlockSpec((bb, r, D_MODEL), lambda b, i, j: (b, i, 0)),
                  pl.BlockSpec((bb, 1, D_MODEL), lambda b, i, j: (b, 0, 0)),
                  pl.BlockSpec((bb, 1, D_MODEL), lambda b, i, j: (b, 0, 1)),
                  pl.BlockSpec((1, D_MODEL), lambda b, i, j: (0, 0)),
                  pl.BlockSpec((D_MODEL, P_WIDTH), lambda b, i, j: (0, 0), pipeline_mode=pl.Buffered(1)),
                  pl.BlockSpec((D_MODEL, LANES), lambda b, i, j: (0, 0))],
        out_specs=[pl.BlockSpec((bb, r, tn), lambda b, i, j: (b, i, j)),
                   pl.BlockSpec((bb, r, LANES), lambda b, i, j: (b, i, 0))],
        out_shape=[jax.ShapeDtypeStruct((nb, rr, P_WIDTH), F32),
                   jax.ShapeDtypeStruct((nb, rr, LANES), F32)],
        scratch_shapes=[pltpu.VMEM((bb * r, D_MODEL), BF16)],
        compiler_params=_params(("arbitrary", "arbitrary", "arbitrary"), FRONT_VMEM_LIMIT),
        name="front_proj",
    )(x, mod, mod, norm_g.reshape(1, -1), w_main, w_dt)


def _alibi_slopes():
    n = N_GROUPS * HEADS
    m = 2.0 ** (-ALIBI_MAX_EXP * np.arange(1, n + 1) / n)
    return m.reshape(N_GROUPS, HEADS).astype(np.float32)


def _prompt_bias(g):
    win, dil = ATT_GROUPS[g]
    span = win // dil
    qi = np.arange(ATT_BLOCK)[:, None]
    kj = np.arange(2 * ATT_BLOCK)[None, :]
    delta = qi + ATT_BLOCK - kj
    valid = (delta >= 0) & (delta <= span)
    slopes = _alibi_slopes()[g]
    alibi = -slopes[:, None, None] * (delta * dil).astype(np.float32)[None]
    later = np.where(valid[None], alibi, np.float32(NEG_INF))
    first = np.where((valid & (kj >= ATT_BLOCK))[None], alibi, np.float32(NEG_INF))
    return np.stack([first, later]).astype(np.float32)


ATT_TILE = ATT_BLOCK * max(d for _, d in ATT_GROUPS)
ATT_UNROLL = 8


def _attn_pair(q, k, v, bias, lo):
    outs, lses = [], []
    for half in range(2):
        qm = jnp.where(lo if half == 0 else jnp.logical_not(lo), q, 0.0).astype(BF16)
        s = _dot_nt(qm, k) + bias(half)
        m = jnp.max(s, axis=-1, keepdims=True)
        e = jnp.exp(s - m)
        l = jnp.sum(e, axis=-1, keepdims=True)
        outs.append(_dot(e.astype(BF16), v) * (1.0 / l))
        lses.append(m + jnp.log(l))
    return jnp.where(lo, outs[0], outs[1]), jnp.where(lo, lses[0], lses[1])


def _attn_prompt_kernel(*refs):
    ins, (bias_ref, att_ref, o_s, lse_s) = refs[:5 * N_GROUPS], refs[5 * N_GROUPS:]
    lo = lax.broadcasted_iota(jnp.int32, (ATT_BLOCK, LANES), 1) < HEAD_DIM
    scale = HEAD_DIM ** -0.5
    lead_var = jnp.minimum(pl.program_id(2), 1)

    for g, (_, d) in enumerate(ATT_GROUPS):
        q_ref, kp_ref, kc_ref, vp_ref, vc_ref = ins[5 * g:5 * g + 5]
        ext = ATT_BLOCK * d
        nb = ATT_TILE // ext

        def rows(start, n, d=d):
            return pl.ds(start, n) if d == 1 else pl.ds(start, n, stride=d)

        def emit(start, o, lse, g=g, rows=rows):
            o_s[g, rows(start, ATT_BLOCK), :] = o
            lse_s[g, rows(start, ATT_BLOCK), :] = lse

        def lead(r, c, g=g, rows=rows, emit=emit, q_ref=q_ref, kp_ref=kp_ref, kc_ref=kc_ref,
                 vp_ref=vp_ref, vc_ref=vc_ref):
            sel = rows(r, ATT_BLOCK)
            k = jnp.concatenate([kp_ref[0, sel, :], kc_ref[0, sel, :]], axis=0).astype(BF16)
            v = jnp.concatenate([vp_ref[0, sel, :], vc_ref[0, sel, :]], axis=0).astype(BF16)
            o, lse = _attn_pair(q_ref[0, sel, :] * scale, k, v,
                                lambda half: bias_ref[g, lead_var, half], lo)
            emit(r, o, lse)
            return c

        def later(n, c, g=g, rows=rows, emit=emit, ext=ext, nb=nb, q_ref=q_ref, kc_ref=kc_ref, vc_ref=vc_ref):
            if d == 1:
                prev = pl.multiple_of(ext * n, ATT_BLOCK)
            else:
                r = n // (nb - 1)
                prev = r + ext * (n - r * (nb - 1))
            k = kc_ref[0, rows(prev, 2 * ATT_BLOCK), :].astype(BF16)
            v = vc_ref[0, rows(prev, 2 * ATT_BLOCK), :].astype(BF16)
            o, lse = _attn_pair(q_ref[0, rows(prev + ext, ATT_BLOCK), :] * scale, k, v,
                                lambda half: bias_ref[g, 1, half], lo)
            emit(prev + ext, o, lse)
            return c

        lax.fori_loop(0, d, lead, 0, unroll=min(d, ATT_UNROLL))
        if nb > 1:
            trips = d * (nb - 1)
            lax.fori_loop(0, trips, later, 0, unroll=max(u for u in range(1, ATT_UNROLL + 2) if trips % u == 0))

    cr = 256

    def combine(i, c):
        sel = pl.ds(pl.multiple_of(i * cr, cr), cr)
        l0, l1, l2 = lse_s[0, sel, :], lse_s[1, sel, :], lse_s[2, sel, :]
        m = jnp.maximum(jnp.maximum(l0, l1), l2)
        e0, e1, e2 = jnp.exp(l0 - m), jnp.exp(l1 - m), jnp.exp(l2 - m)
        att_ref[0, sel, :] = ((e0 * o_s[0, sel, :] + e1 * o_s[1, sel, :] + e2 * o_s[2, sel, :])
                              * (1.0 / (e0 + e1 + e2)))
        return c

    lax.fori_loop(0, ATT_TILE // cr, combine, 0)


def _attn_prompt(P):
    B, T, _ = P.shape
    assert T % ATT_TILE == 0
    pairs = ATT_WIDTH // LANES
    in_specs, args = [], []
    for g, (_, d) in enumerate(ATT_GROUPS):
        ext = ATT_BLOCK * d
        nb = ATT_TILE // ext

        def cur(col, g=g):
            c0 = (col + g * ATT_WIDTH) // LANES
            return pl.BlockSpec((1, ATT_TILE, LANES), lambda p, b, t: (b, t, c0 + p))

        def prev(col, g=g, ext=ext, nb=nb):
            c0 = (col + g * ATT_WIDTH) // LANES
            return pl.BlockSpec((1, ext, LANES), lambda p, b, t: (b, jnp.maximum(t * nb - 1, 0), c0 + p))

        in_specs += [cur(COL_Q), prev(COL_K), cur(COL_K), prev(COL_V), cur(COL_V)]
        args += [P] * 5
    bias = jnp.asarray(np.stack([_prompt_bias(g) for g in range(N_GROUPS)]))
    in_specs.append(pl.BlockSpec((N_GROUPS, 2, 2, ATT_BLOCK, 2 * ATT_BLOCK), lambda p, b, t: (0, 0, p, 0, 0)))
    return pl.pallas_call(
        _attn_prompt_kernel,
        grid=(pairs, B, T // ATT_TILE),
        in_specs=in_specs,
        out_specs=pl.BlockSpec((1, ATT_TILE, LANES), lambda p, b, t: (b, t, p)),
        out_shape=jax.ShapeDtypeStruct((B, T, ATT_WIDTH), F32),
        scratch_shapes=[pltpu.VMEM((N_GROUPS, ATT_TILE, LANES), F32),
                        pltpu.VMEM((N_GROUPS, ATT_TILE, LANES), F32)],
        compiler_params=_params(("arbitrary", "arbitrary", "arbitrary")),
        name="attn_prompt",
    )(*args, bias)


def _sample_bias(g, S):
    win, dil = ATT_GROUPS[g]
    slopes = _alibi_slopes()[g]
    s_idx = np.tile(np.arange(S), HEADS)[:, None]
    slope = np.repeat(slopes, S)[:, None]

    def bias(pos, extra_valid):
        dist = win + s_idx - pos[None, :]
        valid = (dist >= 0) & (dist <= win) & (dist % dil == 0) & extra_valid[None, :]
        return np.where(valid, -slope * dist.astype(np.float32), np.float32(NEG_INF)).astype(np.float32)

    old = bias(np.arange(win), np.ones(win, bool))
    new = bias(win + np.arange(LANES) - (LANES - S), np.arange(LANES) >= LANES - S)
    return old, new


def _attn_sample_work(S, cache_ref, q_ref, k_ref, v_ref, bo_ref, bn_ref, mask_ref,
                      new_ref, o_ref, lse_ref):
    _, rows2, w = cache_ref.shape
    nch = w // LANES
    carry = {}

    def rotated(j):
        return pltpu.roll(cache_ref[0, :, j * LANES:(j + 1) * LANES], LANES - S, 1)

    def new_tile():
        if "new" not in carry:
            kv_new = jnp.concatenate([k_ref[0], v_ref[0]], axis=1)
            carry["new"] = jnp.concatenate([jnp.zeros((LANES - S, rows2), F32), kv_new], axis=0).T
        return carry["new"]

    def shift_chunk(j):
        def run():
            keep = lax.broadcasted_iota(jnp.int32, (rows2, LANES), 1) < LANES - S
            cur = carry.pop("rot") if "rot" in carry else rotated(j)
            nxt = rotated(j + 1) if j + 1 < nch else new_tile()
            new_ref[0, :, j * LANES:(j + 1) * LANES] = jnp.where(keep, cur, nxt)
            carry["rot"] = nxt
        return run

    def attention():
        mask = mask_ref[...]
        q = q_ref[0] * (HEAD_DIM ** -0.5)
        qbd = (jnp.concatenate([q] * HEADS, axis=0) * mask).astype(BF16)
        kt_old = cache_ref[0, 0:ATT_WIDTH, :].astype(BF16)
        vt_old = cache_ref[0, ATT_WIDTH:rows2, :].astype(BF16)
        kt_new = new_tile()[0:ATT_WIDTH].astype(BF16)
        vt_new = new_tile()[ATT_WIDTH:rows2].astype(BF16)
        s_old = _dot(qbd, kt_old) + bo_ref[...]
        s_new = _dot(qbd, kt_new) + bn_ref[...]
        m = jnp.maximum(jnp.max(s_old, axis=-1, keepdims=True), jnp.max(s_new, axis=-1, keepdims=True))
        e_old = jnp.exp(s_old - m)
        e_new = jnp.exp(s_new - m)
        l = jnp.sum(e_old, axis=-1, keepdims=True) + jnp.sum(e_new, axis=-1, keepdims=True)
        o = (_dot_nt(e_old.astype(BF16), vt_old) + _dot_nt(e_new.astype(BF16), vt_new)) * (1.0 / l)
        o = o * mask
        lse = (m + jnp.log(l)) * mask
        o_acc, lse_acc = o[0:S], lse[0:S]
        for h in range(1, HEADS):
            o_acc = o_acc + o[h * S:(h + 1) * S]
            lse_acc = lse_acc + lse[h * S:(h + 1) * S]
        o_ref[0] = o_acc
        lse_ref[0] = lse_acc

    return [shift_chunk(j) for j in range(nch)] + [attention]


class _Part(NamedTuple):
    kernel: Callable
    in_specs: list
    args: list
    out_specs: list
    out_shapes: list
    scratch: list


def _run_parts(parts, grid, name, vmem_limit, interleave):
    n_in = [len(p.in_specs) for p in parts]
    n_out = [len(p.out_specs) for p in parts]
    n_scr = [len(p.scratch) for p in parts]

    def body(*refs):
        ins, outs, scr = refs[:sum(n_in)], refs[sum(n_in):sum(n_in) + sum(n_out)], refs[sum(n_in) + sum(n_out):]

        def refs_of(k):
            take = lambda seq, counts: seq[sum(counts[:k]):sum(counts[:k + 1])]
            return (*take(ins, n_in), *take(outs, n_out), *take(scr, n_scr))

        works = [parts[k].kernel(*refs_of(k)) for k in range(1, len(parts))]
        side = [w[i] for i in range(max(map(len, works), default=0)) for w in works if i < len(w)]
        parts[0].kernel(*refs_of(0), side=side, interleave=interleave)

    flat = lambda field: [x for p in parts for x in getattr(p, field)]
    res = pl.pallas_call(
        body,
        grid=grid,
        in_specs=flat("in_specs"),
        out_specs=flat("out_specs"),
        out_shape=flat("out_shapes"),
        scratch_shapes=flat("scratch"),
        compiler_params=_params(("arbitrary",) * len(grid), vmem_limit),
        name=name,
    )(*flat("args"))
    return [res[sum(n_out[:k]):sum(n_out[:k + 1])] for k in range(len(parts))]


def _attn_sample_part(Ps, cache, g, batch_of):
    B, S, _ = Ps.shape
    w = cache.shape[1]
    cache_t = jnp.transpose(cache, (0, 2, 3, 4, 1)).reshape(B, 2 * ATT_WIDTH, w)
    bo, bn = _sample_bias(g, S)
    mask = (np.arange(HEADS * S)[:, None] // S == np.arange(ATT_WIDTH)[None, :] // HEAD_DIM).astype(np.float32)
    col = lambda c: pl.BlockSpec((1, S, ATT_WIDTH), lambda *ids: (batch_of(*ids), 0, c // ATT_WIDTH + g))
    const = lambda a: pl.BlockSpec(a.shape, lambda *ids: (0, 0))
    out_spec = pl.BlockSpec((1, S, ATT_WIDTH), lambda *ids: (batch_of(*ids), 0, 0))
    win_spec = pl.BlockSpec((1, 2 * ATT_WIDTH, w), lambda *ids: (batch_of(*ids), 0, 0))
    return _Part(
        kernel=functools.partial(_attn_sample_work, S),
        in_specs=[win_spec, col(COL_Q), col(COL_K), col(COL_V), const(bo), const(bn), const(mask)],
        args=[cache_t, Ps, Ps, Ps, jnp.asarray(bo), jnp.asarray(bn), jnp.asarray(mask)],
        out_specs=[win_spec, out_spec, out_spec],
        out_shapes=[jax.ShapeDtypeStruct((B, 2 * ATT_WIDTH, w), F32),
                    jax.ShapeDtypeStruct((B, S, ATT_WIDTH), F32),
                    jax.ShapeDtypeStruct((B, S, ATT_WIDTH), F32)],
        scratch=[])


def _window_from_position_minor(new, w):
    B = new.shape[0]
    return jnp.transpose(new.reshape(B, 2, HEADS, HEAD_DIM, w), (0, 4, 1, 2, 3))


def _ssd_kernel(L, xbc_ref, z_ref, dt_ref, convin_ref, statein_ref, convw_ref, convb_ref,
                dtb_ref, alog_ref, dskip_ref, ng_ref, expand_ref, y_ref, state_ref,
                xpad_s, act_s, y_s, xw_s, fac_s, side=(), interleave=False):
    side = list(side)
    n_stages = SSM_HEADS // 2
    per_stage = -(-len(side) // n_stages) if interleave else 0

    def run_side(n):
        for _ in range(min(n, len(side))):
            side.pop(0)()

    LP = SSD_CHUNK
    pad = CONV_WIDTH - 1
    base = 8

    @pl.when(pl.program_id(1) == 0)
    def _():
        state_ref[...] = statein_ref[...]
        xpad_s[0:base, :] = jnp.zeros((base, CONV_DIM), F32)
        xpad_s[base - pad:base, :] = convin_ref[0]

    xpad_s[base:base + L, :] = xbc_ref[0]
    cw = 512
    for cc in range(CONV_DIM // cw):
        cl = slice(cc * cw, (cc + 1) * cw)
        xfull = xpad_s[:, cl]
        x1 = pltpu.roll(xfull, 1, 0)
        u2 = pltpu.roll(convw_ref[1:2, cl] * xfull + convw_ref[0:1, cl] * x1, 2, 0)
        conv = convb_ref[:, cl] + convw_ref[3:4, cl] * xfull + convw_ref[2:3, cl] * x1 + u2
        act_s[:, cl] = _silu(conv[base:base + L])
    xpad_s[base - pad:base, :] = xbc_ref[0, L - pad:L, :]

    dt = _softplus(dt_ref[0] + dtb_ref[...])
    row = lax.broadcasted_iota(jnp.int32, (LP, LP), 0)
    colm = lax.broadcasted_iota(jnp.int32, (LP, LP), 1)
    a = jnp.where(colm[0:1] < SSM_HEADS, -jnp.exp(alog_ref[...]), 0.0)
    da = _pad_rows(dt * a, LP)
    tril = (row >= colm).astype(F32)
    eye = (row == colm).astype(F32)
    cs = jnp.dot(tril, da, precision=HIGHEST, preferred_element_type=F32)
    cs2 = cs * LOG2E
    cs2_t = _dot_nt(eye, cs2, precision=HIGHEST)
    cs_l = cs2[0:L]
    cs_last = cs2[LP - 1:LP]
    ecs = jnp.exp2(cs_l)
    dte = jnp.exp2(cs_last - cs_l)
    etot = jnp.exp2(cs_last)
    causal = row[0:L] >= colm[0:L]
    lo = lax.broadcasted_iota(jnp.int32, (L, LANES), 1) < SSM_HEAD_DIM

    fac = jnp.concatenate([dt, ecs, dte], axis=0)
    f_hi = fac.astype(BF16)
    f_r = fac - f_hi.astype(F32)
    f_mid = f_r.astype(BF16)
    f_lo = (f_r - f_mid.astype(F32)).astype(BF16)
    pieces = _pad_rows(jnp.concatenate([f_hi, f_mid, f_lo], axis=1), fac_s.shape[0])
    fac_s[...] = _dot(pieces, expand_ref[...])

    hpg = SSM_HEADS // SSM_GROUPS
    gw = hpg * SSM_HEAD_DIM
    for g in range(SSM_GROUPS):
        bg = act_s[:, D_INNER + g * D_STATE:D_INNER + (g + 1) * D_STATE]
        cg = act_s[:, D_INNER + SSM_GROUPS * D_STATE + g * D_STATE:
                   D_INNER + SSM_GROUPS * D_STATE + (g + 1) * D_STATE]
        bg_pad = _pad_rows(bg, LP).astype(BF16)
        cb = _dot_nt(cg.astype(BF16), bg_pad) if L >= 16 else _dot_nt(cg, _pad_rows(bg, LP))
        sg = state_ref[0, g * gw:(g + 1) * gw, :]
        yoff_g = _dot_nt(cg.astype(BF16), sg.astype(BF16)) if L >= 16 else _dot_nt(cg, sg)
        for hq in range(hpg // 2):
            ha = g * hpg + 2 * hq
            sl = slice(ha * SSM_HEAD_DIM, (ha + 2) * SSM_HEAD_DIM)
            run_side(per_stage)
            xdt = act_s[:, sl] * fac_s[0:L, sl]
            xdt_pad = _pad_rows(xdt, LP).astype(BF16)
            ys = []
            for h in (ha, ha + 1):
                seg = cs_l[:, h:h + 1] - cs2_t[h:h + 1, :]
                m = cb * jnp.exp2(jnp.where(causal, seg, -jnp.inf))
                ys.append(_dot(m.astype(BF16), xdt_pad) if L >= 16 else _dot(m, xdt_pad.astype(F32)))
            ydiag = jnp.where(lo, ys[0], ys[1])
            yoff = yoff_g[:, 2 * hq * SSM_HEAD_DIM:(2 * hq + 2) * SSM_HEAD_DIM] * fac_s[L:2 * L, sl]
            y_s[:, sl] = ydiag + yoff
            xw_s[0:L, 2 * hq * SSM_HEAD_DIM:(2 * hq + 2) * SSM_HEAD_DIM] = xdt * fac_s[2 * L:3 * L, sl]
        if L < LP:
            xw_s[L:LP, :] = jnp.zeros((LP - L, gw), F32)
        upd = _dot(xw_s[...].T.astype(BF16), bg_pad)
        for hh in range(hpg):
            h = g * hpg + hh
            rows = slice(h * SSM_HEAD_DIM, (h + 1) * SSM_HEAD_DIM)
            urows = slice(hh * SSM_HEAD_DIM, (hh + 1) * SSM_HEAD_DIM)
            state_ref[0, rows, :] = state_ref[0, rows, :] * etot[0:1, h:h + 1] + upd[urows]

    run_side(len(side))
    for g in range(SSM_GROUPS):
        gl = slice(g * gw, (g + 1) * gw)
        yg = y_s[:, gl] + dskip_ref[:, gl] * act_s[:, gl]
        ug = yg * _silu(z_ref[0, :, gl])
        ms = jnp.mean(ug * ug, axis=-1, keepdims=True)
        y_ref[0, :, gl] = ug * lax.rsqrt(ms + NORM_EPS) * ng_ref[:, gl]


def _ssd_part(P, dt, conv_in, state_in, L, conv_w, conv_b, dt_bias, a_log, d_skip, ssm_norm_g):
    B, T, _ = P.shape
    pad_lanes = lambda v: jnp.pad(v.astype(F32), (0, LANES - SSM_HEADS)).reshape(1, LANES)
    dskip = jnp.repeat(d_skip.astype(F32), SSM_HEAD_DIM).reshape(1, D_INNER)
    const = lambda shape: pl.BlockSpec(shape, lambda b, c: (0, 0))
    spread = np.arange(LANES)[:, None] == np.arange(D_INNER)[None, :] // SSM_HEAD_DIM
    expand = jnp.asarray(np.tile(spread, (3, 1)), BF16)
    fac_rows = -(-3 * L // 16) * 16
    return _Part(
        kernel=functools.partial(_ssd_kernel, L),
        in_specs=[pl.BlockSpec((1, L, CONV_DIM), lambda b, c: (b, c, COL_XBC // CONV_DIM)),
                  pl.BlockSpec((1, L, D_INNER), lambda b, c: (b, c, COL_Z // D_INNER)),
                  pl.BlockSpec((1, L, LANES), lambda b, c: (b, c, 0)),
                  pl.BlockSpec((1, CONV_WIDTH - 1, CONV_DIM), lambda b, c: (b, 0, 0)),
                  pl.BlockSpec((1, D_INNER, D_STATE), lambda b, c: (b, 0, 0)),
                  const((CONV_WIDTH, CONV_DIM)), const((1, CONV_DIM)), const((1, LANES)),
                  const((1, LANES)), const((1, D_INNER)), const((1, D_INNER)),
                  pl.BlockSpec(expand.shape, lambda b, c: (0, 0), pipeline_mode=pl.Buffered(1))],
        out_specs=[pl.BlockSpec((1, L, D_INNER), lambda b, c: (b, c, 0)),
                   pl.BlockSpec((1, D_INNER, D_STATE), lambda b, c: (b, 0, 0))],
        out_shapes=[jax.ShapeDtypeStruct((B, T, D_INNER), F32),
                    jax.ShapeDtypeStruct((B, D_INNER, D_STATE), F32)],
        scratch=[pltpu.VMEM((8 + L, CONV_DIM), F32),
                 pltpu.VMEM((L, CONV_DIM), F32),
                 pltpu.VMEM((L, D_INNER), F32),
                 pltpu.VMEM((SSD_CHUNK, D_INNER // SSM_GROUPS), F32),
                 pltpu.VMEM((fac_rows, D_INNER), F32)],
        args=[P, P, dt, conv_in, state_in.reshape(B, D_INNER, D_STATE), conv_w, conv_b.reshape(1, -1),
              pad_lanes(dt_bias), pad_lanes(a_log), dskip, ssm_norm_g.reshape(1, -1), expand])


def _back_kernel(n_att, *refs):
    att_refs = refs[:n_att]
    (gatt_ref, yssm_ref, ga_ref, gb_ref, x_ref, gate_ref, watt_ref, wssm_ref, wout_ref, fg_ref,
     y_ref) = refs[n_att:]
    bb, r, _ = x_ref.shape
    n = bb * r
    flat = lambda ref: ref[...].reshape(n, ref.shape[-1])
    if n_att == 1:
        att = flat(att_refs[0])
    else:
        os_, ls_ = att_refs[:N_GROUPS], att_refs[N_GROUPS:]
        l0, l1, l2 = flat(ls_[0]), flat(ls_[1]), flat(ls_[2])
        m = jnp.maximum(jnp.maximum(l0, l1), l2)
        e0, e1, e2 = jnp.exp(l0 - m), jnp.exp(l1 - m), jnp.exp(l2 - m)
        att = (e0 * flat(os_[0]) + e1 * flat(os_[1]) + e2 * flat(os_[2])) * (1.0 / (e0 + e1 + e2))
    a_out = _dot((att * _silu(flat(gatt_ref))).astype(BF16), watt_ref[...])
    m_out = _dot(flat(yssm_ref).astype(BF16), wssm_ref[...])
    merged = _sigmoid(flat(ga_ref)) * a_out + _sigmoid(flat(gb_ref)) * m_out
    res = _dot(merged.astype(BF16), wout_ref[...]).reshape(bb, r, D_MODEL)
    xo = x_ref[...] + gate_ref[...] * res
    ms = jnp.mean(xo * xo, axis=-1, keepdims=True)
    y_ref[...] = xo * lax.rsqrt(ms + NORM_EPS) * fg_ref[...]


def _back(x, mod, P, atts, y_ssm, w_att, w_ssm, w_out, final_g, bb, r):
    nb, rr, _ = x.shape
    grid = (nb // bb, rr // r)
    row = lambda width, col: pl.BlockSpec((bb, r, width), lambda b, i: (b, i, col // width))
    const = lambda a: pl.BlockSpec(a.shape, lambda b, i: (0,) * a.ndim)
    fg = final_g.reshape(1, -1)
    return pl.pallas_call(
        functools.partial(_back_kernel, len(atts)),
        grid=grid,
        in_specs=[row(ATT_WIDTH, 0)] * len(atts)
        + [row(ATT_WIDTH, COL_GATT), row(D_INNER, 0), row(D_MODEL, COL_GA), row(D_MODEL, COL_GB),
           row(D_MODEL, 0), pl.BlockSpec((bb, 1, D_MODEL), lambda b, i: (b, 0, 2)),
           const(w_att), const(w_ssm), const(w_out), const(fg)],
        out_specs=row(D_MODEL, 0),
        out_shape=jax.ShapeDtypeStruct(x.shape, F32),
        compiler_params=_params(("arbitrary", "arbitrary")),
        name="back_proj",
    )(*atts, P, y_ssm, P, P, x, mod, w_att, w_ssm, w_out, fg)


def kernel(x_prompt, x_sample, c_prompt, c_sample, cache_kv_w128, cache_kv_w512, cache_kv_w2048,
           state_ssm, state_conv, norm_g, w_ada, b_ada, w_in, conv_w, conv_b, dt_bias, a_log,
           d_skip, ssm_norm_g, w_att_branch, w_ssm_branch, w_out, final_norm_g):
    depth = w_in.shape[0]
    assert depth == 1
    B, T, _ = x_prompt.shape
    Bs, S, _ = x_sample.shape
    caches = (cache_kv_w128, cache_kv_w512, cache_kv_w2048)
    l = 0

    offs = np.cumsum((0,) + IN_SIZES)
    sec = lambda i: w_in[l][:, offs[i]:offs[i + 1]]
    w_main = jnp.concatenate([sec(5), sec(0), sec(1), sec(2), sec(3), sec(4), sec(7), sec(8)], axis=1).astype(BF16)
    w_dt = jnp.pad(sec(6), ((0, 0), (0, LANES - SSM_HEADS))).astype(BF16)
    w_att = w_att_branch[l].astype(BF16)
    w_ssm = w_ssm_branch[l].astype(BF16)
    w_o = w_out[l].astype(BF16)

    n_pad = -(B + Bs) % 8
    c_all = jnp.concatenate([c_prompt, c_sample, jnp.zeros((n_pad, D_MODEL), F32)], axis=0)
    mod = _modulation(c_all, w_ada[l], b_ada[l])
    mod_p = mod[:B].reshape(B, 1, 3 * D_MODEL)
    mod_s = mod[B:B + Bs].reshape(Bs, 1, 3 * D_MODEL)

    ssm_args = (conv_w[l], conv_b[l], dt_bias[l], a_log[l], d_skip[l], ssm_norm_g[l])

    Pp, dtp = _front(x_prompt, mod_p, norm_g[l], w_main, w_dt, bb=1, r=1024, tn=1024)
    Ps, dts = _front(x_sample, mod_s, norm_g[l], w_main, w_dt, bb=32, r=S, tn=1024)

    zeros_state = jnp.zeros((B, SSM_HEADS, SSM_HEAD_DIM, D_STATE), F32)
    zeros_conv = jnp.zeros((B, CONV_WIDTH - 1, CONV_DIM), F32)
    nc = T // SSD_CHUNK
    assert Bs == B * nc
    (y_ssm_p, ssm_p), (new2, o2, lse2) = _run_parts(
        [_ssd_part(Pp, dtp, zeros_conv, zeros_state, SSD_CHUNK, *ssm_args),
         _attn_sample_part(Ps, caches[2][l], 2, lambda b, c: b * nc + c)],
        grid=(B, nc), name="ssd_prompt_window2", vmem_limit=FUSED_VMEM_LIMIT, interleave=True)
    (y_ssm_s, ssm_s), (new0, o0, lse0), (new1, o1, lse1) = _run_parts(
        [_ssd_part(Ps, dts, state_conv[l], state_ssm[l], S, *ssm_args),
         _attn_sample_part(Ps, caches[0][l], 0, lambda b, c: b),
         _attn_sample_part(Ps, caches[1][l], 1, lambda b, c: b)],
        grid=(Bs, 1), name="ssd_sample_window01", vmem_limit=VMEM_LIMIT, interleave=False)
    ssm_p = ssm_p.reshape(zeros_state.shape)
    ssm_s = ssm_s.reshape(state_ssm[l].shape)
    kv_s = [_window_from_position_minor(new, new.shape[-1])[None] for new in (new0, new1, new2)]

    att_p = _attn_prompt(Pp)
    kv_p = []
    for g, (win, _) in enumerate(ATT_GROUPS):
        keep = min(win, T)
        kk = Pp[:, T - keep:, COL_K + g * ATT_WIDTH:COL_K + (g + 1) * ATT_WIDTH]
        vv = Pp[:, T - keep:, COL_V + g * ATT_WIDTH:COL_V + (g + 1) * ATT_WIDTH]
        kv_p.append(jnp.stack([kk.reshape(B, keep, HEADS, HEAD_DIM), vv.reshape(B, keep, HEADS, HEAD_DIM)],
                              axis=2)[None])
    conv_p = Pp[:, T - (CONV_WIDTH - 1):, COL_XBC:COL_XBC + CONV_DIM]
    y_prompt = _back(x_prompt, mod_p, Pp, [att_p], y_ssm_p, w_att, w_ssm, w_o, final_norm_g, bb=1, r=256)

    conv_s = jnp.concatenate([state_conv[l], Ps[:, :, COL_XBC:COL_XBC + CONV_DIM]], axis=1)[:, -(CONV_WIDTH - 1):]
    y_sample = _back(x_sample, mod_s, Ps, [o0, o1, o2, lse0, lse1, lse2], y_ssm_s, w_att, w_ssm, w_o,
                     final_norm_g, bb=32, r=S)

    return (y_prompt, y_sample, kv_p[0], kv_p[1], kv_p[2], ssm_p[None], conv_p[None],
            kv_s[0], kv_s[1], kv_s[2], ssm_s[None], conv_s[None])
```

```python
import functools
from typing import Callable, NamedTuple

import numpy as np
import jax
import jax.numpy as jnp
from jax import lax
from jax.experimental import pallas as pl
from jax.experimental.pallas import tpu as pltpu

F32 = jnp.float32
BF16 = jnp.bfloat16
HIGHEST = lax.Precision.HIGHEST

D_MODEL = 1024
ATT_GROUPS = ((128, 1), (512, 4), (2048, 16))
N_GROUPS = len(ATT_GROUPS)
HEAD_DIM = 64
HEADS = 8
ATT_WIDTH = HEADS * HEAD_DIM
QKV_WIDTH = N_GROUPS * ATT_WIDTH
ATT_BLOCK = 128
ALIBI_MAX_EXP = 8.0
D_INNER = 2048
SSM_HEAD_DIM = 64
SSM_HEADS = 32
SSM_GROUPS = 4
D_STATE = 128
CONV_WIDTH = 4
CONV_DIM = D_INNER + 2 * SSM_GROUPS * D_STATE
SSD_CHUNK = 128
IN_SIZES = (QKV_WIDTH, QKV_WIDTH, QKV_WIDTH, ATT_WIDTH, D_INNER, CONV_DIM, SSM_HEADS, D_MODEL, D_MODEL)
NORM_EPS = 1e-6
NEG_INF = -1e30
LOG2E = 1.4426950408889634

LANES = 128
P_WIDTH = 12288
COL_Q = 0
COL_K = COL_Q + QKV_WIDTH
COL_V = COL_K + QKV_WIDTH
COL_GATT = COL_V + QKV_WIDTH
COL_Z = COL_GATT + ATT_WIDTH
COL_XBC = COL_Z + D_INNER
COL_GA = COL_XBC + CONV_DIM
COL_GB = COL_GA + D_MODEL
SSD_IN_BLOCK = 1024
VMEM_LIMIT = 48 * 1024 * 1024
FRONT_VMEM_LIMIT = 56 * 1024 * 1024
FUSED_VMEM_LIMIT = 58 * 1024 * 1024


def _sigmoid(x):
    return 0.5 * (1.0 + jnp.tanh(0.5 * x))


def _silu(x):
    h = 0.5 * x
    return h + h * jnp.tanh(h)


def _softplus(x):
    return jnp.maximum(x, 0.0) + jnp.log(1.0 + jnp.exp(-jnp.abs(x)))


def _dot(a, b):
    return jnp.dot(a, b, preferred_element_type=F32)


def _dot_nt(a, b, precision=None):
    return lax.dot_general(a, b, (((1,), (1,)), ((), ())), precision=precision,
                           preferred_element_type=F32)


def _pad_rows(x, rows):
    if x.shape[0] == rows:
        return x
    return jnp.concatenate([x, jnp.zeros((rows - x.shape[0],) + x.shape[1:], x.dtype)], axis=0)


def _params(sem, vmem_limit=VMEM_LIMIT):
    return pltpu.CompilerParams(dimension_semantics=sem, vmem_limit_bytes=vmem_limit)


def _mod_kernel(c_ref, w_ref, b_ref, o_ref):
    s = _silu(c_ref[...]).astype(BF16)
    o_ref[...] = _dot(s, w_ref[...].astype(BF16)) + b_ref[...]


def _modulation(c, w_ada, b_ada):
    n = c.shape[0]
    tn = 512
    return pl.pallas_call(
        _mod_kernel,
        grid=(3 * D_MODEL // tn,),
        in_specs=[pl.BlockSpec((n, D_MODEL), lambda j: (0, 0)),
                  pl.BlockSpec((D_MODEL, tn), lambda j: (0, j)),
                  pl.BlockSpec((1, tn), lambda j: (0, j))],
        out_specs=pl.BlockSpec((n, tn), lambda j: (0, j)),
        out_shape=jax.ShapeDtypeStruct((n, 3 * D_MODEL), F32),
        compiler_params=_params(("arbitrary",)),
        name="adaln_mod",
    )(c, w_ada, b_ada.reshape(1, -1))


def _front_kernel(x_ref, shift_ref, scale_ref, g_ref, wa_ref, wb_ref, wdt_ref, out_ref, dt_ref, h_ref):
    bb, r, _ = x_ref.shape

    @pl.when(pl.program_id(2) == 0)
    def _():
        x = x_ref[...]
        ms = jnp.mean(x * x, axis=-1, keepdims=True)
        xn = x * lax.rsqrt(ms + NORM_EPS) * g_ref[...]
        h = xn * (1.0 + scale_ref[...]) + shift_ref[...]
        h2 = h.reshape(bb * r, D_MODEL).astype(BF16)
        h_ref[...] = h2
        dt_ref[...] = _dot(h2, wdt_ref[...]).reshape(bb, r, LANES)

    tn = out_ref.shape[-1]
    j = pl.program_id(2)
    n_head = (P_WIDTH - wb_ref.shape[-1]) // tn

    @pl.when(j < n_head)
    def _():
        cols = pl.ds(pl.multiple_of(j * tn, tn), tn)
        out_ref[...] = _dot(h_ref[...], wa_ref[:, cols]).reshape(out_ref.shape)

    @pl.when(j >= n_head)
    def _():
        cols = pl.ds(pl.multiple_of((j - n_head) * tn, tn), tn)
        out_ref[...] = _dot(h_ref[...], wb_ref[:, cols]).reshape(out_ref.shape)


def _front(x, mod, norm_g, w_a, w_b, w_dt, bb, r, tn):
    nb, rr, _ = x.shape
    assert w_b.shape[1] % tn == 0 and (P_WIDTH - w_b.shape[1]) % tn == 0
    resident = lambda w: pl.BlockSpec(w.shape, lambda b, i, j: (0, 0), pipeline_mode=pl.Buffered(1))
    grid = (nb // bb, rr // r, P_WIDTH // tn)
    return pl.pallas_call(
        _front_kernel,
        grid=grid,
        in_specs=[pl.BlockSpec((bb, r, D_MODEL), lambda b, i, j: (b, i, 0)),
                  pl.BlockSpec((bb, 1, D_MODEL), lambda b, i, j: (b, 0, 0)),
                  pl.BlockSpec((bb, 1, D_MODEL), lambda b, i, j: (b, 0, 1)),
                  pl.BlockSpec((1, D_MODEL), lambda b, i, j: (0, 0)),
                  resident(w_a), resident(w_b),
                  pl.BlockSpec((D_MODEL, LANES), lambda b, i, j: (0, 0))],
        out_specs=[pl.BlockSpec((bb, r, tn), lambda b, i, j: (b, i, j)),
                   pl.BlockSpec((bb, r, LANES), lambda b, i, j: (b, i, 0))],
        out_shape=[jax.ShapeDtypeStruct((nb, rr, P_WIDTH), F32),
                   jax.ShapeDtypeStruct((nb, rr, LANES), F32)],
        scratch_shapes=[pltpu.VMEM((bb * r, D_MODEL), BF16)],
        compiler_params=_params(("arbitrary", "arbitrary", "arbitrary"), FRONT_VMEM_LIMIT),
        name="front_proj",
    )(x, mod, mod, norm_g.reshape(1, -1), w_a, w_b, w_dt)


def _alibi_slopes():
    n = N_GROUPS * HEADS
    m = 2.0 ** (-ALIBI_MAX_EXP * np.arange(1, n + 1) / n)
    return m.reshape(N_GROUPS, HEADS).astype(np.float32)


def _prompt_bias(g):
    win, dil = ATT_GROUPS[g]
    span = win // dil
    qi = np.arange(ATT_BLOCK)[:, None]
    kj = np.arange(2 * ATT_BLOCK)[None, :]
    delta = qi + ATT_BLOCK - kj
    valid = (delta >= 0) & (delta <= span)
    slopes = _alibi_slopes()[g]
    alibi = -slopes[:, None, None] * (delta * dil).astype(np.float32)[None]
    later = np.where(valid[None], alibi, np.float32(NEG_INF))
    first = np.where((valid & (kj >= ATT_BLOCK))[None], alibi, np.float32(NEG_INF))
    return np.stack([first, later]).astype(np.float32)


ATT_TILE = ATT_BLOCK * max(d for _, d in ATT_GROUPS)
ATT_UNROLL = 8


def _attn_blocks(qs, ks, vs, bias, lo):
    hi = jnp.logical_not(lo)
    ss = [_dot_nt(jnp.where(sel, q, 0.0).astype(BF16), k) + bias(half)
          for q, k in zip(qs, ks) for half, sel in enumerate((lo, hi))]
    ms = [jnp.max(s, axis=-1, keepdims=True) for s in ss]
    es = [jnp.exp(s - m) for s, m in zip(ss, ms)]
    ls = [jnp.sum(e, axis=-1, keepdims=True) for e in es]
    os_ = [_dot(e.astype(BF16), vs[i // 2]) * (1.0 / l) for i, (e, l) in enumerate(zip(es, ls))]
    lses = [m + jnp.log(l) for m, l in zip(ms, ls)]
    return [(jnp.where(lo, os_[2 * i], os_[2 * i + 1]), jnp.where(lo, lses[2 * i], lses[2 * i + 1]))
            for i in range(len(qs))]


def _attn_prompt_kernel(*refs):
    ins, (bias_ref, att_ref, o_s, lse_s) = refs[:5 * N_GROUPS], refs[5 * N_GROUPS:]
    lo = lax.broadcasted_iota(jnp.int32, (ATT_BLOCK, LANES), 1) < HEAD_DIM
    scale = HEAD_DIM ** -0.5
    lead_var = jnp.minimum(pl.program_id(2), 1)

    for g, (_, d) in enumerate(ATT_GROUPS):
        q_ref, kp_ref, kc_ref, vp_ref, vc_ref = ins[5 * g:5 * g + 5]
        ext = ATT_BLOCK * d
        nb = ATT_TILE // ext

        def rows(start, n, d=d):
            return pl.ds(start, n) if d == 1 else pl.ds(start, n, stride=d)

        def emit(start, o, lse, g=g, rows=rows):
            o_s[g, rows(start, ATT_BLOCK), :] = o
            lse_s[g, rows(start, ATT_BLOCK), :] = lse

        n_lead = min(d, ATT_UNROLL)
        trips = d * (nb - 1)
        n_later = max(u for u in range(1, ATT_UNROLL + 2) if trips % u == 0) if trips else 0

        def lead(i, c, g=g, rows=rows, emit=emit, n_lead=n_lead, q_ref=q_ref, kp_ref=kp_ref, kc_ref=kc_ref,
                 vp_ref=vp_ref, vc_ref=vc_ref):
            sels = [rows(i * n_lead + u, ATT_BLOCK) for u in range(n_lead)]
            qs = [q_ref[0, sel, :] * scale for sel in sels]
            ks = [jnp.concatenate([kp_ref[0, sel, :], kc_ref[0, sel, :]], axis=0).astype(BF16) for sel in sels]
            vs = [jnp.concatenate([vp_ref[0, sel, :], vc_ref[0, sel, :]], axis=0).astype(BF16) for sel in sels]
            outs = _attn_blocks(qs, ks, vs, lambda half: bias_ref[g, lead_var, half], lo)
            for u, (o, lse) in enumerate(outs):
                emit(i * n_lead + u, o, lse)
            return c

        def later(i, c, g=g, rows=rows, emit=emit, ext=ext, nb=nb, n_later=n_later, q_ref=q_ref,
                  kc_ref=kc_ref, vc_ref=vc_ref):
            prevs = []
            for u in range(n_later):
                n = i * n_later + u
                if d == 1:
                    prevs.append(pl.multiple_of(ext * n, ATT_BLOCK))
                else:
                    r = n // (nb - 1)
                    prevs.append(r + ext * (n - r * (nb - 1)))
            qs = [q_ref[0, rows(p + ext, ATT_BLOCK), :] * scale for p in prevs]
            ks = [kc_ref[0, rows(p, 2 * ATT_BLOCK), :].astype(BF16) for p in prevs]
            vs = [vc_ref[0, rows(p, 2 * ATT_BLOCK), :].astype(BF16) for p in prevs]
            outs = _attn_blocks(qs, ks, vs, lambda half: bias_ref[g, 1, half], lo)
            for p, (o, lse) in zip(prevs, outs):
                emit(p + ext, o, lse)
            return c

        lax.fori_loop(0, d // n_lead, lead, 0)
        if trips:
            lax.fori_loop(0, trips // n_later, later, 0)

    cr = 256

    def combine(i, c):
        sel = pl.ds(pl.multiple_of(i * cr, cr), cr)
        l0, l1, l2 = lse_s[0, sel, :], lse_s[1, sel, :], lse_s[2, sel, :]
        m = jnp.maximum(jnp.maximum(l0, l1), l2)
        e0, e1, e2 = jnp.exp(l0 - m), jnp.exp(l1 - m), jnp.exp(l2 - m)
        att_ref[0, sel, :] = ((e0 * o_s[0, sel, :] + e1 * o_s[1, sel, :] + e2 * o_s[2, sel, :])
                              * (1.0 / (e0 + e1 + e2)))
        return c

    lax.fori_loop(0, ATT_TILE // cr, combine, 0)


def _attn_prompt(P):
    B, T, _ = P.shape
    assert T % ATT_TILE == 0
    pairs = ATT_WIDTH // LANES
    in_specs, args = [], []
    for g, (_, d) in enumerate(ATT_GROUPS):
        ext = ATT_BLOCK * d
        nb = ATT_TILE // ext

        def cur(col, g=g):
            c0 = (col + g * ATT_WIDTH) // LANES
            return pl.BlockSpec((1, ATT_TILE, LANES), lambda p, b, t: (b, t, c0 + p))

        def prev(col, g=g, ext=ext, nb=nb):
            c0 = (col + g * ATT_WIDTH) // LANES
            return pl.BlockSpec((1, ext, LANES), lambda p, b, t: (b, jnp.maximum(t * nb - 1, 0), c0 + p))

        in_specs += [cur(COL_Q), prev(COL_K), cur(COL_K), prev(COL_V), cur(COL_V)]
        args += [P] * 5
    bias = jnp.asarray(np.stack([_prompt_bias(g) for g in range(N_GROUPS)]))
    in_specs.append(pl.BlockSpec((N_GROUPS, 2, 2, ATT_BLOCK, 2 * ATT_BLOCK), lambda p, b, t: (0, 0, p, 0, 0)))
    return pl.pallas_call(
        _attn_prompt_kernel,
        grid=(pairs, B, T // ATT_TILE),
        in_specs=in_specs,
        out_specs=pl.BlockSpec((1, ATT_TILE, LANES), lambda p, b, t: (b, t, p)),
        out_shape=jax.ShapeDtypeStruct((B, T, ATT_WIDTH), F32),
        scratch_shapes=[pltpu.VMEM((N_GROUPS, ATT_TILE, LANES), F32),
                        pltpu.VMEM((N_GROUPS, ATT_TILE, LANES), F32)],
        compiler_params=_params(("arbitrary", "arbitrary", "arbitrary")),
        name="attn_prompt",
    )(*args, bias)


def _sample_bias(g, S):
    win, dil = ATT_GROUPS[g]
    slopes = _alibi_slopes()[g]
    s_idx = np.tile(np.arange(S), HEADS)[:, None]
    slope = np.repeat(slopes, S)[:, None]

    def bias(pos, extra_valid):
        dist = win + s_idx - pos[None, :]
        valid = (dist >= 0) & (dist <= win) & (dist % dil == 0) & extra_valid[None, :]
        return np.where(valid, -slope * dist.astype(np.float32), np.float32(NEG_INF)).astype(np.float32)

    old = bias(np.arange(win), np.ones(win, bool))
    new = bias(win + np.arange(LANES) - (LANES - S), np.arange(LANES) >= LANES - S)
    return old, new


def _attn_sample_work(S, cache_ref, q_ref, k_ref, v_ref, bo_ref, bn_ref, mask_ref,
                      new_ref, o_ref, lse_ref):
    _, rows2, w = cache_ref.shape
    nch = w // LANES
    carry = {}

    def rotated(j):
        return pltpu.roll(cache_ref[0, :, j * LANES:(j + 1) * LANES], LANES - S, 1)

    def new_tile():
        if "new" not in carry:
            kv_new = jnp.concatenate([k_ref[0], v_ref[0]], axis=1)
            carry["new"] = jnp.concatenate([jnp.zeros((LANES - S, rows2), F32), kv_new], axis=0).T
        return carry["new"]

    def shift_chunk(j):
        def run():
            keep = lax.broadcasted_iota(jnp.int32, (rows2, LANES), 1) < LANES - S
            cur = carry.pop("rot") if "rot" in carry else rotated(j)
            nxt = rotated(j + 1) if j + 1 < nch else new_tile()
            new_ref[0, :, j * LANES:(j + 1) * LANES] = jnp.where(keep, cur, nxt)
            carry["rot"] = nxt
        return run

    def attention():
        mask = mask_ref[...]
        q = q_ref[0] * (HEAD_DIM ** -0.5)
        qbd = (jnp.concatenate([q] * HEADS, axis=0) * mask).astype(BF16)
        kt_old = cache_ref[0, 0:ATT_WIDTH, :].astype(BF16)
        vt_old = cache_ref[0, ATT_WIDTH:rows2, :].astype(BF16)
        kt_new = new_tile()[0:ATT_WIDTH].astype(BF16)
        vt_new = new_tile()[ATT_WIDTH:rows2].astype(BF16)
        s_old = _dot(qbd, kt_old) + bo_ref[...]
        s_new = _dot(qbd, kt_new) + bn_ref[...]
        m = jnp.maximum(jnp.max(s_old, axis=-1, keepdims=True), jnp.max(s_new, axis=-1, keepdims=True))
        e_old = jnp.exp(s_old - m)
        e_new = jnp.exp(s_new - m)
        l = jnp.sum(e_old, axis=-1, keepdims=True) + jnp.sum(e_new, axis=-1, keepdims=True)
        o = (_dot_nt(e_old.astype(BF16), vt_old) + _dot_nt(e_new.astype(BF16), vt_new)) * (1.0 / l)
        o = o * mask
        lse = (m + jnp.log(l)) * mask
        o_acc, lse_acc = o[0:S], lse[0:S]
        for h in range(1, HEADS):
            o_acc = o_acc + o[h * S:(h + 1) * S]
            lse_acc = lse_acc + lse[h * S:(h + 1) * S]
        o_ref[0] = o_acc
        lse_ref[0] = lse_acc

    return [shift_chunk(j) for j in range(nch)] + [attention]


class _Part(NamedTuple):
    kernel: Callable
    in_specs: list
    args: list
    out_specs: list
    out_shapes: list
    scratch: list


def _run_parts(parts, grid, name, vmem_limit, interleave):
    n_in = [len(p.in_specs) for p in parts]
    n_out = [len(p.out_specs) for p in parts]
    n_scr = [len(p.scratch) for p in parts]

    def body(*refs):
        ins, outs, scr = refs[:sum(n_in)], refs[sum(n_in):sum(n_in) + sum(n_out)], refs[sum(n_in) + sum(n_out):]

        def refs_of(k):
            take = lambda seq, counts: seq[sum(counts[:k]):sum(counts[:k + 1])]
            return (*take(ins, n_in), *take(outs, n_out), *take(scr, n_scr))

        works = [parts[k].kernel(*refs_of(k)) for k in range(1, len(parts))]
        side = [w[i] for i in range(max(map(len, works), default=0)) for w in works if i < len(w)]
        parts[0].kernel(*refs_of(0), side=side, interleave=interleave)

    flat = lambda field: [x for p in parts for x in getattr(p, field)]
    res = pl.pallas_call(
        body,
        grid=grid,
        in_specs=flat("in_specs"),
        out_specs=flat("out_specs"),
        out_shape=flat("out_shapes"),
        scratch_shapes=flat("scratch"),
        compiler_params=_params(("arbitrary",) * len(grid), vmem_limit),
        name=name,
    )(*flat("args"))
    return [res[sum(n_out[:k]):sum(n_out[:k + 1])] for k in range(len(parts))]


def _attn_sample_part(Ps, cache, g, batch_of):
    B, S, _ = Ps.shape
    w = cache.shape[1]
    cache_t = jnp.transpose(cache, (0, 2, 3, 4, 1)).reshape(B, 2 * ATT_WIDTH, w)
    bo, bn = _sample_bias(g, S)
    mask = (np.arange(HEADS * S)[:, None] // S == np.arange(ATT_WIDTH)[None, :] // HEAD_DIM).astype(np.float32)
    col = lambda c: pl.BlockSpec((1, S, ATT_WIDTH), lambda *ids: (batch_of(*ids), 0, c // ATT_WIDTH + g))
    const = lambda a: pl.BlockSpec(a.shape, lambda *ids: (0, 0))
    out_spec = pl.BlockSpec((1, S, ATT_WIDTH), lambda *ids: (batch_of(*ids), 0, 0))
    win_spec = pl.BlockSpec((1, 2 * ATT_WIDTH, w), lambda *ids: (batch_of(*ids), 0, 0))
    return _Part(
        kernel=functools.partial(_attn_sample_work, S),
        in_specs=[win_spec, col(COL_Q), col(COL_K), col(COL_V), const(bo), const(bn), const(mask)],
        args=[cache_t, Ps, Ps, Ps, jnp.asarray(bo), jnp.asarray(bn), jnp.asarray(mask)],
        out_specs=[win_spec, out_spec, out_spec],
        out_shapes=[jax.ShapeDtypeStruct((B, 2 * ATT_WIDTH, w), F32),
                    jax.ShapeDtypeStruct((B, S, ATT_WIDTH), F32),
                    jax.ShapeDtypeStruct((B, S, ATT_WIDTH), F32)],
        scratch=[])


def _window_from_position_minor(new, w):
    B = new.shape[0]
    return jnp.transpose(new.reshape(B, 2, HEADS, HEAD_DIM, w), (0, 4, 1, 2, 3))


def _kv_tail_kernel(k_ref, v_ref, out_ref):
    out_ref[0, 0:ATT_WIDTH, :] = k_ref[0].T
    out_ref[0, ATT_WIDTH:2 * ATT_WIDTH, :] = v_ref[0].T


def _kv_tail(P, g):
    B, T, _ = P.shape
    keep = min(ATT_GROUPS[g][0], T)
    tr = min(keep, 512)
    assert keep % tr == 0 and (T - keep) % tr == 0
    col = lambda c0: pl.BlockSpec((1, tr, ATT_WIDTH),
                                  lambda b, i: (b, (T - keep) // tr + i, c0 // ATT_WIDTH + g))
    out = pl.pallas_call(
        _kv_tail_kernel,
        grid=(B, keep // tr),
        in_specs=[col(COL_K), col(COL_V)],
        out_specs=pl.BlockSpec((1, 2 * ATT_WIDTH, tr), lambda b, i: (b, 0, i)),
        out_shape=jax.ShapeDtypeStruct((B, 2 * ATT_WIDTH, keep), F32),
        compiler_params=_params(("arbitrary", "arbitrary")),
        name=f"kv_tail_g{g}",
    )(P, P)
    return _window_from_position_minor(out, keep)[None]


def _ssd_kernel(L, *refs, side=(), interleave=False):
    n_xbc, n_z = CONV_DIM // SSD_IN_BLOCK, D_INNER // SSD_IN_BLOCK
    xbc_refs, z_refs = refs[:n_xbc], refs[n_xbc:n_xbc + n_z]
    (dt_ref, convin_ref, statein_ref, convw_ref, convb_ref, dtb_ref, alog_ref, dskip_ref, ng_ref, expand_ref,
     y_ref, state_ref, xpad_s, act_s, y_s, xw_s, fac_s) = refs[n_xbc + n_z:]
    side = list(side)
    n_stages = SSM_HEADS // 2
    per_stage = -(-len(side) // n_stages) if interleave else 0

    def run_side(n):
        for _ in range(min(n, len(side))):
            side.pop(0)()

    LP = SSD_CHUNK
    pad = CONV_WIDTH - 1
    base = 8

    @pl.when(pl.program_id(1) == 0)
    def _():
        state_ref[...] = statein_ref[...]
        xpad_s[0:base, :] = jnp.zeros((base, CONV_DIM), F32)
        xpad_s[base - pad:base, :] = convin_ref[0]

    for i, xr in enumerate(xbc_refs):
        xpad_s[base:base + L, i * SSD_IN_BLOCK:(i + 1) * SSD_IN_BLOCK] = xr[0]
    cw = 512
    for cc in range(CONV_DIM // cw):
        cl = slice(cc * cw, (cc + 1) * cw)
        xfull = xpad_s[:, cl]
        x1 = pltpu.roll(xfull, 1, 0)
        u2 = pltpu.roll(convw_ref[1:2, cl] * xfull + convw_ref[0:1, cl] * x1, 2, 0)
        conv = convb_ref[:, cl] + convw_ref[3:4, cl] * xfull + convw_ref[2:3, cl] * x1 + u2
        act_s[:, cl] = _silu(conv[base:base + L])
    for i, xr in enumerate(xbc_refs):
        xpad_s[base - pad:base, i * SSD_IN_BLOCK:(i + 1) * SSD_IN_BLOCK] = xr[0, L - pad:L, :]

    dt = _softplus(dt_ref[0] + dtb_ref[...])
    row = lax.broadcasted_iota(jnp.int32, (LP, LP), 0)
    colm = lax.broadcasted_iota(jnp.int32, (LP, LP), 1)
    a = jnp.where(colm[0:1] < SSM_HEADS, -jnp.exp(alog_ref[...]), 0.0)
    da = _pad_rows(dt * a, LP)
    tril = (row >= colm).astype(F32)
    eye = (row == colm).astype(F32)
    cs = jnp.dot(tril, da, precision=HIGHEST, preferred_element_type=F32)
    cs2 = cs * LOG2E
    cs2_t = _dot_nt(eye, cs2, precision=HIGHEST)
    cs_l = cs2[0:L]
    cs_last = cs2[LP - 1:LP]
    ecs = jnp.exp2(cs_l)
    dte = jnp.exp2(cs_last - cs_l)
    etot = jnp.exp2(cs_last)
    causal = row[0:L] >= colm[0:L]
    lo = lax.broadcasted_iota(jnp.int32, (L, LANES), 1) < SSM_HEAD_DIM

    fac = jnp.concatenate([dt, ecs, dte], axis=0)
    f_hi = fac.astype(BF16)
    f_r = fac - f_hi.astype(F32)
    f_mid = f_r.astype(BF16)
    f_lo = (f_r - f_mid.astype(F32)).astype(BF16)
    pieces = _pad_rows(jnp.concatenate([f_hi, f_mid, f_lo], axis=1), fac_s.shape[0])
    fac_s[...] = _dot(pieces, expand_ref[...])

    hpg = SSM_HEADS // SSM_GROUPS
    gw = hpg * SSM_HEAD_DIM
    for g in range(SSM_GROUPS):
        bg = act_s[:, D_INNER + g * D_STATE:D_INNER + (g + 1) * D_STATE]
        cg = act_s[:, D_INNER + SSM_GROUPS * D_STATE + g * D_STATE:
                   D_INNER + SSM_GROUPS * D_STATE + (g + 1) * D_STATE]
        bg_pad = _pad_rows(bg, LP).astype(BF16)
        cb = _dot_nt(cg.astype(BF16), bg_pad) if L >= 16 else _dot_nt(cg, _pad_rows(bg, LP))
        sg = state_ref[0, g * gw:(g + 1) * gw, :]
        yoff_g = _dot_nt(cg.astype(BF16), sg.astype(BF16)) if L >= 16 else _dot_nt(cg, sg)
        for hq in range(hpg // 2):
            ha = g * hpg + 2 * hq
            sl = slice(ha * SSM_HEAD_DIM, (ha + 2) * SSM_HEAD_DIM)
            run_side(per_stage)
            xdt = act_s[:, sl] * fac_s[0:L, sl]
            xdt_pad = _pad_rows(xdt, LP).astype(BF16)
            ys = []
            for h in (ha, ha + 1):
                seg = cs_l[:, h:h + 1] - cs2_t[h:h + 1, :]
                m = cb * jnp.exp2(jnp.where(causal, seg, -jnp.inf))
                ys.append(_dot(m.astype(BF16), xdt_pad) if L >= 16 else _dot(m, xdt_pad.astype(F32)))
            ydiag = jnp.where(lo, ys[0], ys[1])
            yoff = yoff_g[:, 2 * hq * SSM_HEAD_DIM:(2 * hq + 2) * SSM_HEAD_DIM] * fac_s[L:2 * L, sl]
            y_s[:, sl] = ydiag + yoff
            xw_s[0:L, 2 * hq * SSM_HEAD_DIM:(2 * hq + 2) * SSM_HEAD_DIM] = xdt * fac_s[2 * L:3 * L, sl]
        if L < LP:
            xw_s[L:LP, :] = jnp.zeros((LP - L, gw), F32)
        upd = _dot(xw_s[...].T.astype(BF16), bg_pad)
        for hh in range(hpg):
            h = g * hpg + hh
            rows = slice(h * SSM_HEAD_DIM, (h + 1) * SSM_HEAD_DIM)
            urows = slice(hh * SSM_HEAD_DIM, (hh + 1) * SSM_HEAD_DIM)
            state_ref[0, rows, :] = state_ref[0, rows, :] * etot[0:1, h:h + 1] + upd[urows]

    run_side(len(side))
    for g in range(SSM_GROUPS):
        gl = slice(g * gw, (g + 1) * gw)
        yg = y_s[:, gl] + dskip_ref[:, gl] * act_s[:, gl]
        zb, zo = divmod(g * gw, SSD_IN_BLOCK)
        ug = yg * _silu(z_refs[zb][0, :, zo:zo + gw])
        ms = jnp.mean(ug * ug, axis=-1, keepdims=True)
        y_ref[0, :, gl] = ug * lax.rsqrt(ms + NORM_EPS) * ng_ref[:, gl]


def _ssd_part(P, dt, conv_in, state_in, L, conv_w, conv_b, dt_bias, a_log, d_skip, ssm_norm_g):
    B, T, _ = P.shape
    pad_lanes = lambda v: jnp.pad(v.astype(F32), (0, LANES - SSM_HEADS)).reshape(1, LANES)
    dskip = jnp.repeat(d_skip.astype(F32), SSM_HEAD_DIM).reshape(1, D_INNER)
    const = lambda shape: pl.BlockSpec(shape, lambda b, c: (0, 0))
    spread = np.arange(LANES)[:, None] == np.arange(D_INNER)[None, :] // SSM_HEAD_DIM
    expand = jnp.asarray(np.tile(spread, (3, 1)), BF16)
    fac_rows = -(-3 * L // 16) * 16
    n_in_blocks = (CONV_DIM + D_INNER) // SSD_IN_BLOCK

    def col_block(col0, i):
        cb = col0 // SSD_IN_BLOCK + i
        return pl.BlockSpec((1, L, SSD_IN_BLOCK), lambda b, c: (b, c, cb))

    return _Part(
        kernel=functools.partial(_ssd_kernel, L),
        in_specs=[col_block(COL_XBC, i) for i in range(CONV_DIM // SSD_IN_BLOCK)]
        + [col_block(COL_Z, i) for i in range(D_INNER // SSD_IN_BLOCK)]
        + [pl.BlockSpec((1, L, LANES), lambda b, c: (b, c, 0)),
                  pl.BlockSpec((1, CONV_WIDTH - 1, CONV_DIM), lambda b, c: (b, 0, 0)),
                  pl.BlockSpec((1, D_INNER, D_STATE), lambda b, c: (b, 0, 0)),
                  const((CONV_WIDTH, CONV_DIM)), const((1, CONV_DIM)), const((1, LANES)),
                  const((1, LANES)), const((1, D_INNER)), const((1, D_INNER)),
                  pl.BlockSpec(expand.shape, lambda b, c: (0, 0), pipeline_mode=pl.Buffered(1))],
        out_specs=[pl.BlockSpec((1, L, D_INNER), lambda b, c: (b, c, 0)),
                   pl.BlockSpec((1, D_INNER, D_STATE), lambda b, c: (b, 0, 0))],
        out_shapes=[jax.ShapeDtypeStruct((B, T, D_INNER), F32),
                    jax.ShapeDtypeStruct((B, D_INNER, D_STATE), F32)],
        scratch=[pltpu.VMEM((8 + L, CONV_DIM), F32),
                 pltpu.VMEM((L, CONV_DIM), F32),
                 pltpu.VMEM((L, D_INNER), F32),
                 pltpu.VMEM((SSD_CHUNK, D_INNER // SSM_GROUPS), F32),
                 pltpu.VMEM((fac_rows, D_INNER), F32)],
        args=[P] * n_in_blocks + [dt, conv_in, state_in.reshape(B, D_INNER, D_STATE), conv_w, conv_b.reshape(1, -1),
              pad_lanes(dt_bias), pad_lanes(a_log), dskip, ssm_norm_g.reshape(1, -1), expand])


def _back_kernel(n_att, *refs):
    att_refs = refs[:n_att]
    (gatt_ref, yssm_ref, ga_ref, gb_ref, x_ref, gate_ref, watt_ref, wssm_ref, wout_ref, fg_ref,
     y_ref) = refs[n_att:]
    bb, r, _ = x_ref.shape
    n = bb * r
    flat = lambda ref: ref[...].reshape(n, ref.shape[-1])
    if n_att == 1:
        att = flat(att_refs[0])
    else:
        os_, ls_ = att_refs[:N_GROUPS], att_refs[N_GROUPS:]
        l0, l1, l2 = flat(ls_[0]), flat(ls_[1]), flat(ls_[2])
        m = jnp.maximum(jnp.maximum(l0, l1), l2)
        e0, e1, e2 = jnp.exp(l0 - m), jnp.exp(l1 - m), jnp.exp(l2 - m)
        att = (e0 * flat(os_[0]) + e1 * flat(os_[1]) + e2 * flat(os_[2])) * (1.0 / (e0 + e1 + e2))
    a_out = _dot((att * _silu(flat(gatt_ref))).astype(BF16), watt_ref[...])
    m_out = _dot(flat(yssm_ref).astype(BF16), wssm_ref[...])
    merged = _sigmoid(flat(ga_ref)) * a_out + _sigmoid(flat(gb_ref)) * m_out
    res = _dot(merged.astype(BF16), wout_ref[...]).reshape(bb, r, D_MODEL)
    xo = x_ref[...] + gate_ref[...] * res
    ms = jnp.mean(xo * xo, axis=-1, keepdims=True)
    y_ref[...] = xo * lax.rsqrt(ms + NORM_EPS) * fg_ref[...]


def _back(x, mod, P, atts, y_ssm, w_att, w_ssm, w_out, final_g, bb, r):
    nb, rr, _ = x.shape
    grid = (nb // bb, rr // r)
    row = lambda width, col: pl.BlockSpec((bb, r, width), lambda b, i: (b, i, col // width))
    const = lambda a: pl.BlockSpec(a.shape, lambda b, i: (0,) * a.ndim)
    fg = final_g.reshape(1, -1)
    return pl.pallas_call(
        functools.partial(_back_kernel, len(atts)),
        grid=grid,
        in_specs=[row(ATT_WIDTH, 0)] * len(atts)
        + [row(ATT_WIDTH, COL_GATT), row(D_INNER, 0), row(D_MODEL, COL_GA), row(D_MODEL, COL_GB),
           row(D_MODEL, 0), pl.BlockSpec((bb, 1, D_MODEL), lambda b, i: (b, 0, 2)),
           const(w_att), const(w_ssm), const(w_out), const(fg)],
        out_specs=row(D_MODEL, 0),
        out_shape=jax.ShapeDtypeStruct(x.shape, F32),
        compiler_params=_params(("arbitrary", "arbitrary")),
        name="back_proj",
    )(*atts, P, y_ssm, P, P, x, mod, w_att, w_ssm, w_out, fg)


def kernel(x_prompt, x_sample, c_prompt, c_sample, cache_kv_w128, cache_kv_w512, cache_kv_w2048,
           state_ssm, state_conv, norm_g, w_ada, b_ada, w_in, conv_w, conv_b, dt_bias, a_log,
           d_skip, ssm_norm_g, w_att_branch, w_ssm_branch, w_out, final_norm_g):
    depth = w_in.shape[0]
    assert depth == 1
    B, T, _ = x_prompt.shape
    Bs, S, _ = x_sample.shape
    caches = (cache_kv_w128, cache_kv_w512, cache_kv_w2048)
    l = 0

    offs = np.cumsum((0,) + IN_SIZES)
    w_a = w_in[l].astype(BF16)
    w_b = w_a[:, offs[7]:]
    w_dt = jnp.pad(w_a[:, offs[6]:offs[7]], ((0, 0), (0, LANES - SSM_HEADS)))
    assert offs[6] == COL_GA and w_b.shape[1] == P_WIDTH - COL_GA
    w_att = w_att_branch[l].astype(BF16)
    w_ssm = w_ssm_branch[l].astype(BF16)
    w_o = w_out[l].astype(BF16)

    n_pad = -(B + Bs) % 8
    c_all = jnp.concatenate([c_prompt, c_sample, jnp.zeros((n_pad, D_MODEL), F32)], axis=0)
    mod = _modulation(c_all, w_ada[l], b_ada[l])
    mod_p = mod[:B].reshape(B, 1, 3 * D_MODEL)
    mod_s = mod[B:B + Bs].reshape(Bs, 1, 3 * D_MODEL)

    ssm_args = (conv_w[l], conv_b[l], dt_bias[l], a_log[l], d_skip[l], ssm_norm_g[l])

    Pp, dtp = _front(x_prompt, mod_p, norm_g[l], w_a, w_b, w_dt, bb=1, r=1024, tn=1024)
    Ps, dts = _front(x_sample, mod_s, norm_g[l], w_a, w_b, w_dt, bb=32, r=S, tn=1024)

    zeros_state = jnp.zeros((B, SSM_HEADS, SSM_HEAD_DIM, D_STATE), F32)
    zeros_conv = jnp.zeros((B, CONV_WIDTH - 1, CONV_DIM), F32)
    nc = T // SSD_CHUNK
    assert Bs == B * nc
    (y_ssm_p, ssm_p), (new2, o2, lse2) = _run_parts(
        [_ssd_part(Pp, dtp, zeros_conv, zeros_state, SSD_CHUNK, *ssm_args),
         _attn_sample_part(Ps, caches[2][l], 2, lambda b, c: b * nc + c)],
        grid=(B, nc), name="ssd_prompt_window2", vmem_limit=FUSED_VMEM_LIMIT, interleave=True)
    (y_ssm_s, ssm_s), (new0, o0, lse0), (new1, o1, lse1) = _run_parts(
        [_ssd_part(Ps, dts, state_conv[l], state_ssm[l], S, *ssm_args),
         _attn_sample_part(Ps, caches[0][l], 0, lambda b, c: b),
         _attn_sample_part(Ps, caches[1][l], 1, lambda b, c: b)],
        grid=(Bs, 1), name="ssd_sample_window01", vmem_limit=VMEM_LIMIT, interleave=False)
    ssm_p = ssm_p.reshape(zeros_state.shape)
    ssm_s = ssm_s.reshape(state_ssm[l].shape)
    kv_s = [_window_from_position_minor(new, new.shape[-1])[None] for new in (new0, new1, new2)]

    att_p = _attn_prompt(Pp)
    kv_p = [_kv_tail(Pp, g) for g in range(N_GROUPS)]
    conv_p = Pp[:, T - (CONV_WIDTH - 1):, COL_XBC:COL_XBC + CONV_DIM]
    y_prompt = _back(x_prompt, mod_p, Pp, [att_p], y_ssm_p, w_att, w_ssm, w_o, final_norm_g, bb=1, r=256)

    conv_s = jnp.concatenate([state_conv[l], Ps[:, :, COL_XBC:COL_XBC + CONV_DIM]], axis=1)[:, -(CONV_WIDTH - 1):]
    y_sample = _back(x_sample, mod_s, Ps, [o0, o1, o2, lse0, lse1, lse2], y_ssm_s, w_att, w_ssm, w_o,
                     final_norm_g, bb=32, r=S)

    return (y_prompt, y_sample, kv_p[0], kv_p[1], kv_p[2], ssm_p[None], conv_p[None],
            kv_s[0], kv_s[1], kv_s[2], ssm_s[None], conv_s[None])
```

```python
import functools
from typing import Callable, NamedTuple

import numpy as np
import jax
import jax.numpy as jnp
from jax import lax
from jax.experimental import pallas as pl
from jax.experimental.pallas import tpu as pltpu

F32 = jnp.float32
BF16 = jnp.bfloat16
HIGHEST = lax.Precision.HIGHEST

D_MODEL = 1024
ATT_GROUPS = ((128, 1), (512, 4), (2048, 16))
N_GROUPS = len(ATT_GROUPS)
HEAD_DIM = 64
HEADS = 8
ATT_WIDTH = HEADS * HEAD_DIM
QKV_WIDTH = N_GROUPS * ATT_WIDTH
ATT_BLOCK = 128
ALIBI_MAX_EXP = 8.0
D_INNER = 2048
SSM_HEAD_DIM = 64
SSM_HEADS = 32
SSM_GROUPS = 4
D_STATE = 128
CONV_WIDTH = 4
CONV_DIM = D_INNER + 2 * SSM_GROUPS * D_STATE
SSD_CHUNK = 128
IN_SIZES = (QKV_WIDTH, QKV_WIDTH, QKV_WIDTH, ATT_WIDTH, D_INNER, CONV_DIM, SSM_HEADS, D_MODEL, D_MODEL)
NORM_EPS = 1e-6
NEG_INF = -1e30
LOG2E = 1.4426950408889634

LANES = 128
P_WIDTH = 12288
COL_Q = 0
COL_K = COL_Q + QKV_WIDTH
COL_V = COL_K + QKV_WIDTH
COL_GATT = COL_V + QKV_WIDTH
COL_Z = COL_GATT + ATT_WIDTH
COL_XBC = COL_Z + D_INNER
COL_GA = COL_XBC + CONV_DIM
COL_GB = COL_GA + D_MODEL
SSD_IN_BLOCK = 1024
SAMPLE_SEQS_PER_STEP = 2
VMEM_LIMIT = 48 * 1024 * 1024
FRONT_VMEM_LIMIT = 56 * 1024 * 1024
FUSED_VMEM_LIMIT = 58 * 1024 * 1024


def _sigmoid(x):
    return 0.5 * (1.0 + jnp.tanh(0.5 * x))


def _silu(x):
    h = 0.5 * x
    return h + h * jnp.tanh(h)


def _softplus(x):
    return jnp.maximum(x, 0.0) + jnp.log(1.0 + jnp.exp(-jnp.abs(x)))


def _dot(a, b):
    return jnp.dot(a, b, preferred_element_type=F32)


def _dot_nt(a, b, precision=None):
    return lax.dot_general(a, b, (((1,), (1,)), ((), ())), precision=precision,
                           preferred_element_type=F32)


def _pad_rows(x, rows):
    if x.shape[0] == rows:
        return x
    return jnp.concatenate([x, jnp.zeros((rows - x.shape[0],) + x.shape[1:], x.dtype)], axis=0)


def _params(sem, vmem_limit=VMEM_LIMIT):
    return pltpu.CompilerParams(dimension_semantics=sem, vmem_limit_bytes=vmem_limit)


def _mod_kernel(c_ref, w_ref, b_ref, o_ref):
    s = _silu(c_ref[...]).astype(BF16)
    o_ref[...] = _dot(s, w_ref[...].astype(BF16)) + b_ref[...]


def _modulation(c, w_ada, b_ada):
    n = c.shape[0]
    tn = 512
    return pl.pallas_call(
        _mod_kernel,
        grid=(3 * D_MODEL // tn,),
        in_specs=[pl.BlockSpec((n, D_MODEL), lambda j: (0, 0)),
                  pl.BlockSpec((D_MODEL, tn), lambda j: (0, j)),
                  pl.BlockSpec((1, tn), lambda j: (0, j))],
        out_specs=pl.BlockSpec((n, tn), lambda j: (0, j)),
        out_shape=jax.ShapeDtypeStruct((n, 3 * D_MODEL), F32),
        compiler_params=_params(("arbitrary",)),
        name="adaln_mod",
    )(c, w_ada, b_ada.reshape(1, -1))


def _front_kernel(x_ref, shift_ref, scale_ref, g_ref, wa_ref, wb_ref, wdt_ref, out_ref, dt_ref, h_ref):
    bb, r, _ = x_ref.shape

    @pl.when(pl.program_id(2) == 0)
    def _():
        x = x_ref[...]
        ms = jnp.mean(x * x, axis=-1, keepdims=True)
        xn = x * lax.rsqrt(ms + NORM_EPS) * g_ref[...]
        h = xn * (1.0 + scale_ref[...]) + shift_ref[...]
        h2 = h.reshape(bb * r, D_MODEL).astype(BF16)
        h_ref[...] = h2
        dt_ref[...] = _dot(h2, wdt_ref[...]).reshape(bb, r, LANES)

    tn = out_ref.shape[-1]
    j = pl.program_id(2)
    n_head = (P_WIDTH - wb_ref.shape[-1]) // tn

    @pl.when(j < n_head)
    def _():
        cols = pl.ds(pl.multiple_of(j * tn, tn), tn)
        out_ref[...] = _dot(h_ref[...], wa_ref[:, cols]).reshape(out_ref.shape)

    @pl.when(j >= n_head)
    def _():
        cols = pl.ds(pl.multiple_of((j - n_head) * tn, tn), tn)
        out_ref[...] = _dot(h_ref[...], wb_ref[:, cols]).reshape(out_ref.shape)


def _front(x, mod, norm_g, w_a, w_b, w_dt, bb, r, tn):
    nb, rr, _ = x.shape
    assert w_b.shape[1] % tn == 0 and (P_WIDTH - w_b.shape[1]) % tn == 0
    resident = lambda w: pl.BlockSpec(w.shape, lambda b, i, j: (0, 0), pipeline_mode=pl.Buffered(1))
    grid = (nb // bb, rr // r, P_WIDTH // tn)
    return pl.pallas_call(
        _front_kernel,
        grid=grid,
        in_specs=[pl.BlockSpec((bb, r, D_MODEL), lambda b, i, j: (b, i, 0)),
                  pl.BlockSpec((bb, 1, D_MODEL), lambda b, i, j: (b, 0, 0)),
                  pl.BlockSpec((bb, 1, D_MODEL), lambda b, i, j: (b, 0, 1)),
                  pl.BlockSpec((1, D_MODEL), lambda b, i, j: (0, 0)),
                  resident(w_a), resident(w_b),
                  pl.BlockSpec((D_MODEL, LANES), lambda b, i, j: (0, 0))],
        out_specs=[pl.BlockSpec((bb, r, tn), lambda b, i, j: (b, i, j)),
                   pl.BlockSpec((bb, r, LANES), lambda b, i, j: (b, i, 0))],
        out_shape=[jax.ShapeDtypeStruct((nb, rr, P_WIDTH), F32),
                   jax.ShapeDtypeStruct((nb, rr, LANES), F32)],
        scratch_shapes=[pltpu.VMEM((bb * r, D_MODEL), BF16)],
        compiler_params=_params(("arbitrary", "arbitrary", "arbitrary"), FRONT_VMEM_LIMIT),
        name="front_proj",
    )(x, mod, mod, norm_g.reshape(1, -1), w_a, w_b, w_dt)


def _alibi_slopes():
    n = N_GROUPS * HEADS
    m = 2.0 ** (-ALIBI_MAX_EXP * np.arange(1, n + 1) / n)
    return m.reshape(N_GROUPS, HEADS).astype(np.float32)


def _prompt_bias(g):
    win, dil = ATT_GROUPS[g]
    span = win // dil
    qi = np.arange(ATT_BLOCK)[:, None]
    kj = np.arange(2 * ATT_BLOCK)[None, :]
    delta = qi + ATT_BLOCK - kj
    valid = (delta >= 0) & (delta <= span)
    slopes = _alibi_slopes()[g]
    alibi = -slopes[:, None, None] * (delta * dil).astype(np.float32)[None]
    later = np.where(valid[None], alibi, np.float32(NEG_INF))
    first = np.where((valid & (kj >= ATT_BLOCK))[None], alibi, np.float32(NEG_INF))
    return np.stack([first, later]).astype(np.float32)


ATT_TILE = ATT_BLOCK * max(d for _, d in ATT_GROUPS)
ATT_UNROLL = 8


def _attn_blocks(qs, ks, vs, bias, lo):
    hi = jnp.logical_not(lo)
    ss = [_dot_nt(jnp.where(sel, q, 0.0).astype(BF16), k) + bias(half)
          for q, k in zip(qs, ks) for half, sel in enumerate((lo, hi))]
    ms = [jnp.max(s, axis=-1, keepdims=True) for s in ss]
    es = [jnp.exp(s - m) for s, m in zip(ss, ms)]
    ls = [jnp.sum(e, axis=-1, keepdims=True) for e in es]
    os_ = [_dot(e.astype(BF16), vs[i // 2]) * (1.0 / l) for i, (e, l) in enumerate(zip(es, ls))]
    lses = [m + jnp.log(l) for m, l in zip(ms, ls)]
    return [(jnp.where(lo, os_[2 * i], os_[2 * i + 1]), jnp.where(lo, lses[2 * i], lses[2 * i + 1]))
            for i in range(len(qs))]


def _attn_prompt_kernel(*refs):
    ins, (bias_ref, att_ref, o_s, lse_s) = refs[:5 * N_GROUPS], refs[5 * N_GROUPS:]
    lo = lax.broadcasted_iota(jnp.int32, (ATT_BLOCK, LANES), 1) < HEAD_DIM
    scale = HEAD_DIM ** -0.5
    lead_var = jnp.minimum(pl.program_id(2), 1)

    for g, (_, d) in enumerate(ATT_GROUPS):
        q_ref, kp_ref, kc_ref, vp_ref, vc_ref = ins[5 * g:5 * g + 5]
        ext = ATT_BLOCK * d
        nb = ATT_TILE // ext

        def rows(start, n, d=d):
            return pl.ds(start, n) if d == 1 else pl.ds(start, n, stride=d)

        def emit(start, o, lse, g=g, rows=rows):
            o_s[g, rows(start, ATT_BLOCK), :] = o
            lse_s[g, rows(start, ATT_BLOCK), :] = lse

        n_lead = min(d, ATT_UNROLL)
        trips = d * (nb - 1)
        n_later = max(u for u in range(1, ATT_UNROLL + 2) if trips % u == 0) if trips else 0

        def lead(i, c, g=g, rows=rows, emit=emit, n_lead=n_lead, q_ref=q_ref, kp_ref=kp_ref, kc_ref=kc_ref,
                 vp_ref=vp_ref, vc_ref=vc_ref):
            sels = [rows(i * n_lead + u, ATT_BLOCK) for u in range(n_lead)]
            qs = [q_ref[0, sel, :] * scale for sel in sels]
            ks = [jnp.concatenate([kp_ref[0, sel, :], kc_ref[0, sel, :]], axis=0).astype(BF16) for sel in sels]
            vs = [jnp.concatenate([vp_ref[0, sel, :], vc_ref[0, sel, :]], axis=0).astype(BF16) for sel in sels]
            outs = _attn_blocks(qs, ks, vs, lambda half: bias_ref[g, lead_var, half], lo)
            for u, (o, lse) in enumerate(outs):
                emit(i * n_lead + u, o, lse)
            return c

        def later(i, c, g=g, rows=rows, emit=emit, ext=ext, nb=nb, n_later=n_later, q_ref=q_ref,
                  kc_ref=kc_ref, vc_ref=vc_ref):
            prevs = []
            for u in range(n_later):
                n = i * n_later + u
                if d == 1:
                    prevs.append(pl.multiple_of(ext * n, ATT_BLOCK))
                else:
                    r = n // (nb - 1)
                    prevs.append(r + ext * (n - r * (nb - 1)))
            qs = [q_ref[0, rows(p + ext, ATT_BLOCK), :] * scale for p in prevs]
            ks = [kc_ref[0, rows(p, 2 * ATT_BLOCK), :].astype(BF16) for p in prevs]
            vs = [vc_ref[0, rows(p, 2 * ATT_BLOCK), :].astype(BF16) for p in prevs]
            outs = _attn_blocks(qs, ks, vs, lambda half: bias_ref[g, 1, half], lo)
            for p, (o, lse) in zip(prevs, outs):
                emit(p + ext, o, lse)
            return c

        lax.fori_loop(0, d // n_lead, lead, 0)
        if trips:
            lax.fori_loop(0, trips // n_later, later, 0)

    cr = 256

    def combine(i, c):
        sel = pl.ds(pl.multiple_of(i * cr, cr), cr)
        l0, l1, l2 = lse_s[0, sel, :], lse_s[1, sel, :], lse_s[2, sel, :]
        m = jnp.maximum(jnp.maximum(l0, l1), l2)
        e0, e1, e2 = jnp.exp(l0 - m), jnp.exp(l1 - m), jnp.exp(l2 - m)
        att_ref[0, sel, :] = ((e0 * o_s[0, sel, :] + e1 * o_s[1, sel, :] + e2 * o_s[2, sel, :])
                              * (1.0 / (e0 + e1 + e2)))
        return c

    lax.fori_loop(0, ATT_TILE // cr, combine, 0)


def _attn_prompt(P):
    B, T, _ = P.shape
    assert T % ATT_TILE == 0
    pairs = ATT_WIDTH // LANES
    in_specs, args = [], []
    for g, (_, d) in enumerate(ATT_GROUPS):
        ext = ATT_BLOCK * d
        nb = ATT_TILE // ext

        def cur(col, g=g):
            c0 = (col + g * ATT_WIDTH) // LANES
            return pl.BlockSpec((1, ATT_TILE, LANES), lambda p, b, t: (b, t, c0 + p))

        def prev(col, g=g, ext=ext, nb=nb):
            c0 = (col + g * ATT_WIDTH) // LANES
            return pl.BlockSpec((1, ext, LANES), lambda p, b, t: (b, jnp.maximum(t * nb - 1, 0), c0 + p))

        in_specs += [cur(COL_Q), prev(COL_K), cur(COL_K), prev(COL_V), cur(COL_V)]
        args += [P] * 5
    bias = jnp.asarray(np.stack([_prompt_bias(g) for g in range(N_GROUPS)]))
    in_specs.append(pl.BlockSpec((N_GROUPS, 2, 2, ATT_BLOCK, 2 * ATT_BLOCK), lambda p, b, t: (0, 0, p, 0, 0)))
    return pl.pallas_call(
        _attn_prompt_kernel,
        grid=(pairs, B, T // ATT_TILE),
        in_specs=in_specs,
        out_specs=pl.BlockSpec((1, ATT_TILE, LANES), lambda p, b, t: (b, t, p)),
        out_shape=jax.ShapeDtypeStruct((B, T, ATT_WIDTH), F32),
        scratch_shapes=[pltpu.VMEM((N_GROUPS, ATT_TILE, LANES), F32),
                        pltpu.VMEM((N_GROUPS, ATT_TILE, LANES), F32)],
        compiler_params=_params(("arbitrary", "arbitrary", "arbitrary")),
        name="attn_prompt",
    )(*args, bias)


def _sample_bias(g, S):
    win, dil = ATT_GROUPS[g]
    slopes = _alibi_slopes()[g]
    s_idx = np.tile(np.arange(S), HEADS)[:, None]
    slope = np.repeat(slopes, S)[:, None]

    def bias(pos, extra_valid):
        dist = win + s_idx - pos[None, :]
        valid = (dist >= 0) & (dist <= win) & (dist % dil == 0) & extra_valid[None, :]
        return np.where(valid, -slope * dist.astype(np.float32), np.float32(NEG_INF)).astype(np.float32)

    old = bias(np.arange(win), np.ones(win, bool))
    new = bias(win + np.arange(LANES) - (LANES - S), np.arange(LANES) >= LANES - S)
    return old, new


def _attn_sample_work(S, cache_ref, q_ref, k_ref, v_ref, bo_ref, bn_ref, mask_ref,
                      new_ref, o_ref, lse_ref):
    nq = cache_ref.shape[0]
    per_seq = [_attn_sample_seq_work(S, sq, cache_ref, q_ref, k_ref, v_ref, bo_ref, bn_ref, mask_ref,
                                     new_ref, o_ref, lse_ref) for sq in range(nq)]
    return [w[i] for i in range(len(per_seq[0])) for w in per_seq]


def _attn_sample_seq_work(S, sq, cache_ref, q_ref, k_ref, v_ref, bo_ref, bn_ref, mask_ref,
                          new_ref, o_ref, lse_ref):
    _, rows2, w = cache_ref.shape
    nch = w // LANES
    carry = {}

    def rotated(j):
        return pltpu.roll(cache_ref[sq, :, j * LANES:(j + 1) * LANES], LANES - S, 1)

    def new_tile():
        if "new" not in carry:
            kv_new = jnp.concatenate([k_ref[sq], v_ref[sq]], axis=1)
            carry["new"] = jnp.concatenate([jnp.zeros((LANES - S, rows2), F32), kv_new], axis=0).T
        return carry["new"]

    def shift_chunk(j):
        def run():
            keep = lax.broadcasted_iota(jnp.int32, (rows2, LANES), 1) < LANES - S
            cur = carry.pop("rot") if "rot" in carry else rotated(j)
            nxt = rotated(j + 1) if j + 1 < nch else new_tile()
            new_ref[sq, :, j * LANES:(j + 1) * LANES] = jnp.where(keep, cur, nxt)
            carry["rot"] = nxt
        return run

    def attention():
        mask = mask_ref[...]
        q = q_ref[sq] * (HEAD_DIM ** -0.5)
        qbd = (jnp.concatenate([q] * HEADS, axis=0) * mask).astype(BF16)
        kt_old = cache_ref[sq, 0:ATT_WIDTH, :].astype(BF16)
        vt_old = cache_ref[sq, ATT_WIDTH:rows2, :].astype(BF16)
        kt_new = new_tile()[0:ATT_WIDTH].astype(BF16)
        vt_new = new_tile()[ATT_WIDTH:rows2].astype(BF16)
        s_old = _dot(qbd, kt_old) + bo_ref[...]
        s_new = _dot(qbd, kt_new) + bn_ref[...]
        m = jnp.maximum(jnp.max(s_old, axis=-1, keepdims=True), jnp.max(s_new, axis=-1, keepdims=True))
        e_old = jnp.exp(s_old - m)
        e_new = jnp.exp(s_new - m)
        l = jnp.sum(e_old, axis=-1, keepdims=True) + jnp.sum(e_new, axis=-1, keepdims=True)
        o = (_dot_nt(e_old.astype(BF16), vt_old) + _dot_nt(e_new.astype(BF16), vt_new)) * (1.0 / l)
        o = o * mask
        lse = (m + jnp.log(l)) * mask
        o_acc, lse_acc = o[0:S], lse[0:S]
        for h in range(1, HEADS):
            o_acc = o_acc + o[h * S:(h + 1) * S]
            lse_acc = lse_acc + lse[h * S:(h + 1) * S]
        o_ref[sq] = o_acc
        lse_ref[sq] = lse_acc

    return [shift_chunk(j) for j in range(nch)] + [attention]


class _Part(NamedTuple):
    kernel: Callable
    in_specs: list
    args: list
    out_specs: list
    out_shapes: list
    scratch: list


def _run_parts(parts, grid, name, vmem_limit, interleave):
    n_in = [len(p.in_specs) for p in parts]
    n_out = [len(p.out_specs) for p in parts]
    n_scr = [len(p.scratch) for p in parts]

    def body(*refs):
        ins, outs, scr = refs[:sum(n_in)], refs[sum(n_in):sum(n_in) + sum(n_out)], refs[sum(n_in) + sum(n_out):]

        def refs_of(k):
            take = lambda seq, counts: seq[sum(counts[:k]):sum(counts[:k + 1])]
            return (*take(ins, n_in), *take(outs, n_out), *take(scr, n_scr))

        works = [parts[k].kernel(*refs_of(k)) for k in range(1, len(parts))]
        side = [w[i] for i in range(max(map(len, works), default=0)) for w in works if i < len(w)]
        parts[0].kernel(*refs_of(0), side=side, interleave=interleave)

    flat = lambda field: [x for p in parts for x in getattr(p, field)]
    res = pl.pallas_call(
        body,
        grid=grid,
        in_specs=flat("in_specs"),
        out_specs=flat("out_specs"),
        out_shape=flat("out_shapes"),
        scratch_shapes=flat("scratch"),
        compiler_params=_params(("arbitrary",) * len(grid), vmem_limit),
        name=name,
    )(*flat("args"))
    return [res[sum(n_out[:k]):sum(n_out[:k + 1])] for k in range(len(parts))]


def _attn_sample_part(Ps, cache, g, nq, batch_of):
    B, S, _ = Ps.shape
    w = cache.shape[1]
    assert B % nq == 0
    cache_t = jnp.transpose(cache, (0, 2, 3, 4, 1)).reshape(B, 2 * ATT_WIDTH, w)
    bo, bn = _sample_bias(g, S)
    mask = (np.arange(HEADS * S)[:, None] // S == np.arange(ATT_WIDTH)[None, :] // HEAD_DIM).astype(np.float32)
    col = lambda c: pl.BlockSpec((nq, S, ATT_WIDTH), lambda *ids: (batch_of(*ids), 0, c // ATT_WIDTH + g))
    const = lambda a: pl.BlockSpec(a.shape, lambda *ids: (0, 0))
    out_spec = pl.BlockSpec((nq, S, ATT_WIDTH), lambda *ids: (batch_of(*ids), 0, 0))
    win_spec = pl.BlockSpec((nq, 2 * ATT_WIDTH, w), lambda *ids: (batch_of(*ids), 0, 0))
    return _Part(
        kernel=functools.partial(_attn_sample_work, S),
        in_specs=[win_spec, col(COL_Q), col(COL_K), col(COL_V), const(bo), const(bn), const(mask)],
        args=[cache_t, Ps, Ps, Ps, jnp.asarray(bo), jnp.asarray(bn), jnp.asarray(mask)],
        out_specs=[win_spec, out_spec, out_spec],
        out_shapes=[jax.ShapeDtypeStruct((B, 2 * ATT_WIDTH, w), F32),
                    jax.ShapeDtypeStruct((B, S, ATT_WIDTH), F32),
                    jax.ShapeDtypeStruct((B, S, ATT_WIDTH), F32)],
        scratch=[])


def _window_from_position_minor(new, w):
    B = new.shape[0]
    return jnp.transpose(new.reshape(B, 2, HEADS, HEAD_DIM, w), (0, 4, 1, 2, 3))


def _kv_tail_kernel(k_ref, v_ref, out_ref):
    out_ref[0, 0:ATT_WIDTH, :] = k_ref[0].T
    out_ref[0, ATT_WIDTH:2 * ATT_WIDTH, :] = v_ref[0].T


def _kv_tail(P, g):
    B, T, _ = P.shape
    keep = min(ATT_GROUPS[g][0], T)
    tr = min(keep, 512)
    assert keep % tr == 0 and (T - keep) % tr == 0
    col = lambda c0: pl.BlockSpec((1, tr, ATT_WIDTH),
                                  lambda b, i: (b, (T - keep) // tr + i, c0 // ATT_WIDTH + g))
    out = pl.pallas_call(
        _kv_tail_kernel,
        grid=(B, keep // tr),
        in_specs=[col(COL_K), col(COL_V)],
        out_specs=pl.BlockSpec((1, 2 * ATT_WIDTH, tr), lambda b, i: (b, 0, i)),
        out_shape=jax.ShapeDtypeStruct((B, 2 * ATT_WIDTH, keep), F32),
        compiler_params=_params(("arbitrary", "arbitrary")),
        name=f"kv_tail_g{g}",
    )(P, P)
    return _window_from_position_minor(out, keep)[None]


def _ssd_kernel(L, *refs, side=(), interleave=False):
    n_xbc, n_z = CONV_DIM // SSD_IN_BLOCK, D_INNER // SSD_IN_BLOCK
    xbc_refs, z_refs = refs[:n_xbc], refs[n_xbc:n_xbc + n_z]
    (dt_ref, convin_ref, statein_ref, convw_ref, convb_ref, dtb_ref, alog_ref, dskip_ref, ng_ref, expand_ref,
     y_ref, state_ref, xpad_s, act_s, y_s, xw_s, fac_s) = refs[n_xbc + n_z:]
    side = list(side)
    n_stages = SSM_HEADS // 2
    per_stage = -(-len(side) // n_stages) if interleave else 0

    def run_side(n):
        for _ in range(min(n, len(side))):
            side.pop(0)()

    LP = SSD_CHUNK
    pad = CONV_WIDTH - 1
    base = 8
    hpg = SSM_HEADS // SSM_GROUPS
    gw = hpg * SSM_HEAD_DIM

    def sequence(q):
        @pl.when(pl.program_id(1) == 0)
        def _():
            state_ref[q] = statein_ref[q]
            xpad_s[q, 0:base, :] = jnp.zeros((base, CONV_DIM), F32)
            xpad_s[q, base - pad:base, :] = convin_ref[q]

        for i, xr in enumerate(xbc_refs):
            xpad_s[q, base:base + L, i * SSD_IN_BLOCK:(i + 1) * SSD_IN_BLOCK] = xr[q]
        cw = 512
        for cc in range(CONV_DIM // cw):
            cl = slice(cc * cw, (cc + 1) * cw)
            xfull = xpad_s[q, :, cl]
            x1 = pltpu.roll(xfull, 1, 0)
            u2 = pltpu.roll(convw_ref[1:2, cl] * xfull + convw_ref[0:1, cl] * x1, 2, 0)
            conv = convb_ref[:, cl] + convw_ref[3:4, cl] * xfull + convw_ref[2:3, cl] * x1 + u2
            act_s[q, :, cl] = _silu(conv[base:base + L])
        for i, xr in enumerate(xbc_refs):
            xpad_s[q, base - pad:base, i * SSD_IN_BLOCK:(i + 1) * SSD_IN_BLOCK] = xr[q, L - pad:L, :]
        yield

        dt = _softplus(dt_ref[q] + dtb_ref[...])
        row = lax.broadcasted_iota(jnp.int32, (LP, LP), 0)
        colm = lax.broadcasted_iota(jnp.int32, (LP, LP), 1)
        a = jnp.where(colm[0:1] < SSM_HEADS, -jnp.exp(alog_ref[...]), 0.0)
        da = _pad_rows(dt * a, LP)
        tril = (row >= colm).astype(F32)
        eye = (row == colm).astype(F32)
        cs = jnp.dot(tril, da, precision=HIGHEST, preferred_element_type=F32)
        cs2 = cs * LOG2E
        cs2_t = _dot_nt(eye, cs2, precision=HIGHEST)
        cs_l = cs2[0:L]
        cs_last = cs2[LP - 1:LP]
        ecs = jnp.exp2(cs_l)
        dte = jnp.exp2(cs_last - cs_l)
        etot = jnp.exp2(cs_last)
        causal = row[0:L] >= colm[0:L]
        lo = lax.broadcasted_iota(jnp.int32, (L, LANES), 1) < SSM_HEAD_DIM

        fac = jnp.concatenate([dt, ecs, dte], axis=0)
        f_hi = fac.astype(BF16)
        f_r = fac - f_hi.astype(F32)
        f_mid = f_r.astype(BF16)
        f_lo = (f_r - f_mid.astype(F32)).astype(BF16)
        pieces = _pad_rows(jnp.concatenate([f_hi, f_mid, f_lo], axis=1), fac_s.shape[1])
        fac_s[q] = _dot(pieces, expand_ref[...])
        yield

        for g in range(SSM_GROUPS):
            bg = act_s[q, :, D_INNER + g * D_STATE:D_INNER + (g + 1) * D_STATE]
            cg = act_s[q, :, D_INNER + SSM_GROUPS * D_STATE + g * D_STATE:
                       D_INNER + SSM_GROUPS * D_STATE + (g + 1) * D_STATE]
            bg_pad = _pad_rows(bg, LP).astype(BF16)
            cb = _dot_nt(cg.astype(BF16), bg_pad) if L >= 16 else _dot_nt(cg, _pad_rows(bg, LP))
            sg = state_ref[q, g * gw:(g + 1) * gw, :]
            yoff_g = _dot_nt(cg.astype(BF16), sg.astype(BF16)) if L >= 16 else _dot_nt(cg, sg)
            for hq in range(hpg // 2):
                ha = g * hpg + 2 * hq
                sl = slice(ha * SSM_HEAD_DIM, (ha + 2) * SSM_HEAD_DIM)
                xdt = act_s[q, :, sl] * fac_s[q, 0:L, sl]
                xdt_pad = _pad_rows(xdt, LP).astype(BF16)
                ys = []
                for h in (ha, ha + 1):
                    seg = cs_l[:, h:h + 1] - cs2_t[h:h + 1, :]
                    m = cb * jnp.exp2(jnp.where(causal, seg, -jnp.inf))
                    ys.append(_dot(m.astype(BF16), xdt_pad) if L >= 16 else _dot(m, xdt_pad.astype(F32)))
                ydiag = jnp.where(lo, ys[0], ys[1])
                yoff = yoff_g[:, 2 * hq * SSM_HEAD_DIM:(2 * hq + 2) * SSM_HEAD_DIM] * fac_s[q, L:2 * L, sl]
                y_s[q, :, sl] = ydiag + yoff
                xw_s[q, 0:L, 2 * hq * SSM_HEAD_DIM:(2 * hq + 2) * SSM_HEAD_DIM] = xdt * fac_s[q, 2 * L:3 * L, sl]
                yield
            if L < LP:
                xw_s[q, L:LP, :] = jnp.zeros((LP - L, gw), F32)
            upd = _dot(xw_s[q].T.astype(BF16), bg_pad)
            for hh in range(hpg):
                h = g * hpg + hh
                rows = slice(h * SSM_HEAD_DIM, (h + 1) * SSM_HEAD_DIM)
                urows = slice(hh * SSM_HEAD_DIM, (hh + 1) * SSM_HEAD_DIM)
                state_ref[q, rows, :] = state_ref[q, rows, :] * etot[0:1, h:h + 1] + upd[urows]

        for g in range(SSM_GROUPS):
            gl = slice(g * gw, (g + 1) * gw)
            yg = y_s[q, :, gl] + dskip_ref[:, gl] * act_s[q, :, gl]
            zb, zo = divmod(g * gw, SSD_IN_BLOCK)
            ug = yg * _silu(z_refs[zb][q, :, zo:zo + gw])
            ms = jnp.mean(ug * ug, axis=-1, keepdims=True)
            y_ref[q, :, gl] = (ug * lax.rsqrt(ms + NORM_EPS) * ng_ref[:, gl]).astype(y_ref.dtype)

    running = [sequence(q) for q in range(dt_ref.shape[0])]
    stage = 0
    while running:
        for gen in list(running):
            if next(gen, "done") == "done":
                running.remove(gen)
        if stage >= 2:
            run_side(per_stage)
        stage += 1
    run_side(len(side))


def _ssd_part(P, dt, conv_in, state_in, L, nq, y_dtype, conv_w, conv_b, dt_bias, a_log, d_skip, ssm_norm_g):
    B, T, _ = P.shape
    assert B % nq == 0
    pad_lanes = lambda v: jnp.pad(v.astype(F32), (0, LANES - SSM_HEADS)).reshape(1, LANES)
    dskip = jnp.repeat(d_skip.astype(F32), SSM_HEAD_DIM).reshape(1, D_INNER)
    const = lambda shape: pl.BlockSpec(shape, lambda b, c: (0, 0))
    spread = np.arange(LANES)[:, None] == np.arange(D_INNER)[None, :] // SSM_HEAD_DIM
    expand = jnp.asarray(np.tile(spread, (3, 1)), BF16)
    fac_rows = -(-3 * L // 16) * 16
    n_in_blocks = (CONV_DIM + D_INNER) // SSD_IN_BLOCK

    def col_block(col0, i):
        cb = col0 // SSD_IN_BLOCK + i
        return pl.BlockSpec((nq, L, SSD_IN_BLOCK), lambda b, c: (b, c, cb))

    return _Part(
        kernel=functools.partial(_ssd_kernel, L),
        in_specs=[col_block(COL_XBC, i) for i in range(CONV_DIM // SSD_IN_BLOCK)]
        + [col_block(COL_Z, i) for i in range(D_INNER // SSD_IN_BLOCK)]
        + [pl.BlockSpec((nq, L, LANES), lambda b, c: (b, c, 0)),
                  pl.BlockSpec((nq, CONV_WIDTH - 1, CONV_DIM), lambda b, c: (b, 0, 0)),
                  pl.BlockSpec((nq, D_INNER, D_STATE), lambda b, c: (b, 0, 0)),
                  const((CONV_WIDTH, CONV_DIM)), const((1, CONV_DIM)), const((1, LANES)),
                  const((1, LANES)), const((1, D_INNER)), const((1, D_INNER)),
                  pl.BlockSpec(expand.shape, lambda b, c: (0, 0), pipeline_mode=pl.Buffered(1))],
        out_specs=[pl.BlockSpec((nq, L, D_INNER), lambda b, c: (b, c, 0)),
                   pl.BlockSpec((nq, D_INNER, D_STATE), lambda b, c: (b, 0, 0))],
        out_shapes=[jax.ShapeDtypeStruct((B, T, D_INNER), y_dtype),
                    jax.ShapeDtypeStruct((B, D_INNER, D_STATE), F32)],
        scratch=[pltpu.VMEM((nq, 8 + L, CONV_DIM), F32),
                 pltpu.VMEM((nq, L, CONV_DIM), F32),
                 pltpu.VMEM((nq, L, D_INNER), F32),
                 pltpu.VMEM((nq, SSD_CHUNK, D_INNER // SSM_GROUPS), F32),
                 pltpu.VMEM((nq, fac_rows, D_INNER), F32)],
        args=[P] * n_in_blocks + [dt, conv_in, state_in.reshape(B, D_INNER, D_STATE), conv_w, conv_b.reshape(1, -1),
              pad_lanes(dt_bias), pad_lanes(a_log), dskip, ssm_norm_g.reshape(1, -1), expand])


def _back_kernel(n_att, *refs):
    att_refs = refs[:n_att]
    (gatt_ref, yssm_ref, ga_ref, gb_ref, x_ref, gate_ref, watt_ref, wssm_ref, wout_ref, fg_ref,
     y_ref) = refs[n_att:]
    bb, r, _ = x_ref.shape
    n = bb * r
    flat = lambda ref: ref[...].reshape(n, ref.shape[-1])
    if n_att == 1:
        att = flat(att_refs[0])
    else:
        os_, ls_ = att_refs[:N_GROUPS], att_refs[N_GROUPS:]
        l0, l1, l2 = flat(ls_[0]), flat(ls_[1]), flat(ls_[2])
        m = jnp.maximum(jnp.maximum(l0, l1), l2)
        e0, e1, e2 = jnp.exp(l0 - m), jnp.exp(l1 - m), jnp.exp(l2 - m)
        att = (e0 * flat(os_[0]) + e1 * flat(os_[1]) + e2 * flat(os_[2])) * (1.0 / (e0 + e1 + e2))
    a_out = _dot((att * _silu(flat(gatt_ref))).astype(BF16), watt_ref[...])
    m_out = _dot(flat(yssm_ref).astype(BF16), wssm_ref[...])
    merged = _sigmoid(flat(ga_ref)) * a_out + _sigmoid(flat(gb_ref)) * m_out
    res = _dot(merged.astype(BF16), wout_ref[...]).reshape(bb, r, D_MODEL)
    xo = x_ref[...] + gate_ref[...] * res
    ms = jnp.mean(xo * xo, axis=-1, keepdims=True)
    y_ref[...] = xo * lax.rsqrt(ms + NORM_EPS) * fg_ref[...]


def _back(x, mod, P, atts, y_ssm, w_att, w_ssm, w_out, final_g, bb, r):
    nb, rr, _ = x.shape
    grid = (nb // bb, rr // r)
    row = lambda width, col: pl.BlockSpec((bb, r, width), lambda b, i: (b, i, col // width))
    const = lambda a: pl.BlockSpec(a.shape, lambda b, i: (0,) * a.ndim)
    fg = final_g.reshape(1, -1)
    return pl.pallas_call(
        functools.partial(_back_kernel, len(atts)),
        grid=grid,
        in_specs=[row(ATT_WIDTH, 0)] * len(atts)
        + [row(ATT_WIDTH, COL_GATT), row(D_INNER, 0), row(D_MODEL, COL_GA), row(D_MODEL, COL_GB),
           row(D_MODEL, 0), pl.BlockSpec((bb, 1, D_MODEL), lambda b, i: (b, 0, 2)),
           const(w_att), const(w_ssm), const(w_out), const(fg)],
        out_specs=row(D_MODEL, 0),
        out_shape=jax.ShapeDtypeStruct(x.shape, F32),
        compiler_params=_params(("arbitrary", "arbitrary")),
        name="back_proj",
    )(*atts, P, y_ssm, P, P, x, mod, w_att, w_ssm, w_out, fg)


def kernel(x_prompt, x_sample, c_prompt, c_sample, cache_kv_w128, cache_kv_w512, cache_kv_w2048,
           state_ssm, state_conv, norm_g, w_ada, b_ada, w_in, conv_w, conv_b, dt_bias, a_log,
           d_skip, ssm_norm_g, w_att_branch, w_ssm_branch, w_out, final_norm_g):
    depth = w_in.shape[0]
    assert depth == 1
    B, T, _ = x_prompt.shape
    Bs, S, _ = x_sample.shape
    caches = (cache_kv_w128, cache_kv_w512, cache_kv_w2048)
    l = 0

    offs = np.cumsum((0,) + IN_SIZES)
    w_a = w_in[l].astype(BF16)
    w_b = w_a[:, offs[7]:]
    w_dt = jnp.pad(w_a[:, offs[6]:offs[7]], ((0, 0), (0, LANES - SSM_HEADS)))
    assert offs[6] == COL_GA and w_b.shape[1] == P_WIDTH - COL_GA
    w_att = w_att_branch[l].astype(BF16)
    w_ssm = w_ssm_branch[l].astype(BF16)
    w_o = w_out[l].astype(BF16)

    n_pad = -(B + Bs) % 8
    c_all = jnp.concatenate([c_prompt, c_sample, jnp.zeros((n_pad, D_MODEL), F32)], axis=0)
    mod = _modulation(c_all, w_ada[l], b_ada[l])
    mod_p = mod[:B].reshape(B, 1, 3 * D_MODEL)
    mod_s = mod[B:B + Bs].reshape(Bs, 1, 3 * D_MODEL)

    ssm_args = (conv_w[l], conv_b[l], dt_bias[l], a_log[l], d_skip[l], ssm_norm_g[l])

    Pp, dtp = _front(x_prompt, mod_p, norm_g[l], w_a, w_b, w_dt, bb=1, r=1024, tn=1024)
    Ps, dts = _front(x_sample, mod_s, norm_g[l], w_a, w_b, w_dt, bb=32, r=S, tn=1024)

    zeros_state = jnp.zeros((B, SSM_HEADS, SSM_HEAD_DIM, D_STATE), F32)
    zeros_conv = jnp.zeros((B, CONV_WIDTH - 1, CONV_DIM), F32)
    nc = T // SSD_CHUNK
    assert Bs == B * nc
    (y_ssm_p, ssm_p), (new2, o2, lse2) = _run_parts(
        [_ssd_part(Pp, dtp, zeros_conv, zeros_state, SSD_CHUNK, 1, BF16, *ssm_args),
         _attn_sample_part(Ps, caches[2][l], 2, 1, lambda b, c: b * nc + c)],
        grid=(B, nc), name="ssd_prompt_window2", vmem_limit=FUSED_VMEM_LIMIT, interleave=True)
    nq = SAMPLE_SEQS_PER_STEP
    (y_ssm_s, ssm_s), (new0, o0, lse0), (new1, o1, lse1) = _run_parts(
        [_ssd_part(Ps, dts, state_conv[l], state_ssm[l], S, nq, F32, *ssm_args),
         _attn_sample_part(Ps, caches[0][l], 0, nq, lambda b, c: b),
         _attn_sample_part(Ps, caches[1][l], 1, nq, lambda b, c: b)],
        grid=(Bs // nq, 1), name="ssd_sample_window01", vmem_limit=VMEM_LIMIT, interleave=True)
    ssm_p = ssm_p.reshape(zeros_state.shape)
    ssm_s = ssm_s.reshape(state_ssm[l].shape)
    kv_s = [_window_from_position_minor(new, new.shape[-1])[None] for new in (new0, new1, new2)]

    att_p = _attn_prompt(Pp)
    kv_p = [_kv_tail(Pp, g) for g in range(N_GROUPS)]
    conv_p = Pp[:, T - (CONV_WIDTH - 1):, COL_XBC:COL_XBC + CONV_DIM]
    y_prompt = _back(x_prompt, mod_p, Pp, [att_p], y_ssm_p, w_att, w_ssm, w_o, final_norm_g, bb=1, r=512)

    conv_s = jnp.concatenate([state_conv[l], Ps[:, :, COL_XBC:COL_XBC + CONV_DIM]], axis=1)[:, -(CONV_WIDTH - 1):]
    y_sample = _back(x_sample, mod_s, Ps, [o0, o1, o2, lse0, lse1, lse2], y_ssm_s, w_att, w_ssm, w_o,
                     final_norm_g, bb=32, r=S)

    return (y_prompt, y_sample, kv_p[0], kv_p[1], kv_p[2], ssm_p[None], conv_p[None],
            kv_s[0], kv_s[1], kv_s[2], ssm_s[None], conv_s[None])
```

```python
import functools
from typing import Callable, NamedTuple

import numpy as np
import jax
import jax.numpy as jnp
from jax import lax
from jax.experimental import pallas as pl
from jax.experimental.pallas import tpu as pltpu

F32 = jnp.float32
BF16 = jnp.bfloat16
HIGHEST = lax.Precision.HIGHEST

D_MODEL = 1024
ATT_GROUPS = ((128, 1), (512, 4), (2048, 16))
N_GROUPS = len(ATT_GROUPS)
HEAD_DIM = 64
HEADS = 8
ATT_WIDTH = HEADS * HEAD_DIM
QKV_WIDTH = N_GROUPS * ATT_WIDTH
ATT_BLOCK = 128
ALIBI_MAX_EXP = 8.0
D_INNER = 2048
SSM_HEAD_DIM = 64
SSM_HEADS = 32
SSM_GROUPS = 4
D_STATE = 128
CONV_WIDTH = 4
CONV_DIM = D_INNER + 2 * SSM_GROUPS * D_STATE
SSD_CHUNK = 128
IN_SIZES = (QKV_WIDTH, QKV_WIDTH, QKV_WIDTH, ATT_WIDTH, D_INNER, CONV_DIM, SSM_HEADS, D_MODEL, D_MODEL)
NORM_EPS = 1e-6
NEG_INF = -1e30
LOG2E = 1.4426950408889634

LANES = 128
P_WIDTH = 12288
COL_Q = 0
COL_K = COL_Q + QKV_WIDTH
COL_V = COL_K + QKV_WIDTH
COL_GATT = COL_V + QKV_WIDTH
COL_Z = COL_GATT + ATT_WIDTH
COL_XBC = COL_Z + D_INNER
COL_GA = COL_XBC + CONV_DIM
COL_GB = COL_GA + D_MODEL
SSD_IN_BLOCK = 1024
SAMPLE_SEQS_PER_STEP = 2
VMEM_LIMIT = 48 * 1024 * 1024
FRONT_VMEM_LIMIT = 56 * 1024 * 1024
FUSED_VMEM_LIMIT = 58 * 1024 * 1024


def _sigmoid(x):
    return 0.5 * (1.0 + jnp.tanh(0.5 * x))


def _silu(x):
    h = 0.5 * x
    return h + h * jnp.tanh(h)


def _softplus(x):
    return jnp.maximum(x, 0.0) + jnp.log(1.0 + jnp.exp(-jnp.abs(x)))


def _dot(a, b):
    return jnp.dot(a, b, preferred_element_type=F32)


def _dot_nt(a, b, precision=None):
    return lax.dot_general(a, b, (((1,), (1,)), ((), ())), precision=precision,
                           preferred_element_type=F32)


def _pad_rows(x, rows):
    if x.shape[0] == rows:
        return x
    return jnp.concatenate([x, jnp.zeros((rows - x.shape[0],) + x.shape[1:], x.dtype)], axis=0)


def _params(sem, vmem_limit=VMEM_LIMIT):
    return pltpu.CompilerParams(dimension_semantics=sem, vmem_limit_bytes=vmem_limit)


def _mod_kernel(c_ref, w_ref, b_ref, o_ref):
    s = _silu(c_ref[...]).astype(BF16)
    o_ref[...] = _dot(s, w_ref[...].astype(BF16)) + b_ref[...]


def _modulation(c, w_ada, b_ada):
    n = c.shape[0]
    tn = 512
    return pl.pallas_call(
        _mod_kernel,
        grid=(3 * D_MODEL // tn,),
        in_specs=[pl.BlockSpec((n, D_MODEL), lambda j: (0, 0)),
                  pl.BlockSpec((D_MODEL, tn), lambda j: (0, j)),
                  pl.BlockSpec((1, tn), lambda j: (0, j))],
        out_specs=pl.BlockSpec((n, tn), lambda j: (0, j)),
        out_shape=jax.ShapeDtypeStruct((n, 3 * D_MODEL), F32),
        compiler_params=_params(("arbitrary",)),
        name="adaln_mod",
    )(c, w_ada, b_ada.reshape(1, -1))


def _front_kernel(x_ref, shift_ref, scale_ref, g_ref, wa_ref, wb_ref, wdt_ref, out_ref, dt_ref, h_ref):
    bb, r, _ = x_ref.shape

    @pl.when(pl.program_id(2) == 0)
    def _():
        x = x_ref[...]
        ms = jnp.mean(x * x, axis=-1, keepdims=True)
        xn = x * lax.rsqrt(ms + NORM_EPS) * g_ref[...]
        h = xn * (1.0 + scale_ref[...]) + shift_ref[...]
        h2 = h.reshape(bb * r, D_MODEL).astype(BF16)
        h_ref[...] = h2
        dt_ref[...] = _dot(h2, wdt_ref[...]).reshape(bb, r, LANES)

    tn = out_ref.shape[-1]
    j = pl.program_id(2)
    n_head = (P_WIDTH - wb_ref.shape[-1]) // tn

    @pl.when(j < n_head)
    def _():
        cols = pl.ds(pl.multiple_of(j * tn, tn), tn)
        out_ref[...] = _dot(h_ref[...], wa_ref[:, cols]).reshape(out_ref.shape)

    @pl.when(j >= n_head)
    def _():
        cols = pl.ds(pl.multiple_of((j - n_head) * tn, tn), tn)
        out_ref[...] = _dot(h_ref[...], wb_ref[:, cols]).reshape(out_ref.shape)


def _front(x, mod, norm_g, w_a, w_b, w_dt, bb, r, tn):
    nb, rr, _ = x.shape
    assert w_b.shape[1] % tn == 0 and (P_WIDTH - w_b.shape[1]) % tn == 0
    resident = lambda w: pl.BlockSpec(w.shape, lambda b, i, j: (0, 0), pipeline_mode=pl.Buffered(1))
    grid = (nb // bb, rr // r, P_WIDTH // tn)
    return pl.pallas_call(
        _front_kernel,
        grid=grid,
        in_specs=[pl.BlockSpec((bb, r, D_MODEL), lambda b, i, j: (b, i, 0)),
                  pl.BlockSpec((bb, 1, D_MODEL), lambda b, i, j: (b, 0, 0)),
                  pl.BlockSpec((bb, 1, D_MODEL), lambda b, i, j: (b, 0, 1)),
                  pl.BlockSpec((1, D_MODEL), lambda b, i, j: (0, 0)),
                  resident(w_a), resident(w_b),
                  pl.BlockSpec((D_MODEL, LANES), lambda b, i, j: (0, 0))],
        out_specs=[pl.BlockSpec((bb, r, tn), lambda b, i, j: (b, i, j)),
                   pl.BlockSpec((bb, r, LANES), lambda b, i, j: (b, i, 0))],
        out_shape=[jax.ShapeDtypeStruct((nb, rr, P_WIDTH), F32),
                   jax.ShapeDtypeStruct((nb, rr, LANES), F32)],
        scratch_shapes=[pltpu.VMEM((bb * r, D_MODEL), BF16)],
        compiler_params=_params(("arbitrary", "arbitrary", "arbitrary"), FRONT_VMEM_LIMIT),
        name="front_proj",
    )(x, mod, mod, norm_g.reshape(1, -1), w_a, w_b, w_dt)


def _alibi_slopes():
    n = N_GROUPS * HEADS
    m = 2.0 ** (-ALIBI_MAX_EXP * np.arange(1, n + 1) / n)
    return m.reshape(N_GROUPS, HEADS).astype(np.float32)


def _prompt_bias(g):
    win, dil = ATT_GROUPS[g]
    span = win // dil
    qi = np.arange(ATT_BLOCK)[:, None]
    kj = np.arange(2 * ATT_BLOCK)[None, :]
    delta = qi + ATT_BLOCK - kj
    valid = (delta >= 0) & (delta <= span)
    slopes = _alibi_slopes()[g]
    alibi = -slopes[:, None, None] * (delta * dil).astype(np.float32)[None]
    later = np.where(valid[None], alibi, np.float32(NEG_INF))
    first = np.where((valid & (kj >= ATT_BLOCK))[None], alibi, np.float32(NEG_INF))
    return np.stack([first, later]).astype(np.float32)


ATT_TILE = ATT_BLOCK * max(d for _, d in ATT_GROUPS)
ATT_UNROLL = 4


def _attn_blocks(qs, ks, vs, bias, lo):
    hi = jnp.logical_not(lo)
    ss = [_dot_nt(jnp.where(sel, q, 0.0).astype(BF16), k) + bias(half)
          for q, k in zip(qs, ks) for half, sel in enumerate((lo, hi))]
    ms = [jnp.max(s, axis=-1, keepdims=True) for s in ss]
    es = [jnp.exp(s - m) for s, m in zip(ss, ms)]
    ls = [jnp.sum(e, axis=-1, keepdims=True) for e in es]
    os_ = [_dot(e.astype(BF16), vs[i // 2]) * (1.0 / l) for i, (e, l) in enumerate(zip(es, ls))]
    lses = [m + jnp.log(l) for m, l in zip(ms, ls)]
    return [(jnp.where(lo, os_[2 * i], os_[2 * i + 1]), jnp.where(lo, lses[2 * i], lses[2 * i + 1]))
            for i in range(len(qs))]


def _attn_prompt_kernel(*refs):
    ins, (bias_ref, att_ref, o_s, lse_s) = refs[:5 * N_GROUPS], refs[5 * N_GROUPS:]
    lo = lax.broadcasted_iota(jnp.int32, (ATT_BLOCK, LANES), 1) < HEAD_DIM
    scale = HEAD_DIM ** -0.5
    lead_var = jnp.minimum(pl.program_id(2), 1)

    for g, (_, d) in enumerate(ATT_GROUPS):
        q_ref, kp_ref, kc_ref, vp_ref, vc_ref = ins[5 * g:5 * g + 5]
        ext = ATT_BLOCK * d
        nb = ATT_TILE // ext

        def rows(start, n, d=d):
            return pl.ds(start, n) if d == 1 else pl.ds(start, n, stride=d)

        def emit(start, o, lse, g=g, rows=rows):
            o_s[g, rows(start, ATT_BLOCK), :] = o
            lse_s[g, rows(start, ATT_BLOCK), :] = lse

        n_lead = min(d, ATT_UNROLL)
        trips = d * (nb - 1)
        n_later = max(u for u in range(1, ATT_UNROLL + 2) if trips % u == 0) if trips else 0

        def lead(i, c, g=g, rows=rows, emit=emit, n_lead=n_lead, q_ref=q_ref, kp_ref=kp_ref, kc_ref=kc_ref,
                 vp_ref=vp_ref, vc_ref=vc_ref):
            sels = [rows(i * n_lead + u, ATT_BLOCK) for u in range(n_lead)]
            qs = [q_ref[0, sel, :] * scale for sel in sels]
            ks = [jnp.concatenate([kp_ref[0, sel, :], kc_ref[0, sel, :]], axis=0).astype(BF16) for sel in sels]
            vs = [jnp.concatenate([vp_ref[0, sel, :], vc_ref[0, sel, :]], axis=0).astype(BF16) for sel in sels]
            outs = _attn_blocks(qs, ks, vs, lambda half: bias_ref[g, lead_var, half], lo)
            for u, (o, lse) in enumerate(outs):
                emit(i * n_lead + u, o, lse)
            return c

        def later(i, c, g=g, rows=rows, emit=emit, ext=ext, nb=nb, n_later=n_later, q_ref=q_ref,
                  kc_ref=kc_ref, vc_ref=vc_ref):
            prevs = []
            for u in range(n_later):
                n = i * n_later + u
                if d == 1:
                    prevs.append(pl.multiple_of(ext * n, ATT_BLOCK))
                else:
                    r = n // (nb - 1)
                    prevs.append(r + ext * (n - r * (nb - 1)))
            qs = [q_ref[0, rows(p + ext, ATT_BLOCK), :] * scale for p in prevs]
            ks = [kc_ref[0, rows(p, 2 * ATT_BLOCK), :].astype(BF16) for p in prevs]
            vs = [vc_ref[0, rows(p, 2 * ATT_BLOCK), :].astype(BF16) for p in prevs]
            outs = _attn_blocks(qs, ks, vs, lambda half: bias_ref[g, 1, half], lo)
            for p, (o, lse) in zip(prevs, outs):
                emit(p + ext, o, lse)
            return c

        lax.fori_loop(0, d // n_lead, lead, 0)
        if trips:
            lax.fori_loop(0, trips // n_later, later, 0)

    cr = 256

    def combine(i, c):
        sel = pl.ds(pl.multiple_of(i * cr, cr), cr)
        l0, l1, l2 = lse_s[0, sel, :], lse_s[1, sel, :], lse_s[2, sel, :]
        m = jnp.maximum(jnp.maximum(l0, l1), l2)
        e0, e1, e2 = jnp.exp(l0 - m), jnp.exp(l1 - m), jnp.exp(l2 - m)
        att_ref[0, sel, :] = ((e0 * o_s[0, sel, :] + e1 * o_s[1, sel, :] + e2 * o_s[2, sel, :])
                              * (1.0 / (e0 + e1 + e2)))
        return c

    lax.fori_loop(0, ATT_TILE // cr, combine, 0)


def _attn_prompt(P):
    B, T, _ = P.shape
    assert T % ATT_TILE == 0
    pairs = ATT_WIDTH // LANES
    in_specs, args = [], []
    for g, (_, d) in enumerate(ATT_GROUPS):
        ext = ATT_BLOCK * d
        nb = ATT_TILE // ext

        def cur(col, g=g):
            c0 = (col + g * ATT_WIDTH) // LANES
            return pl.BlockSpec((1, ATT_TILE, LANES), lambda p, b, t: (b, t, c0 + p))

        def prev(col, g=g, ext=ext, nb=nb):
            c0 = (col + g * ATT_WIDTH) // LANES
            return pl.BlockSpec((1, ext, LANES), lambda p, b, t: (b, jnp.maximum(t * nb - 1, 0), c0 + p))

        in_specs += [cur(COL_Q), prev(COL_K), cur(COL_K), prev(COL_V), cur(COL_V)]
        args += [P] * 5
    bias = jnp.asarray(np.stack([_prompt_bias(g) for g in range(N_GROUPS)]))
    in_specs.append(pl.BlockSpec((N_GROUPS, 2, 2, ATT_BLOCK, 2 * ATT_BLOCK), lambda p, b, t: (0, 0, p, 0, 0)))
    return pl.pallas_call(
        _attn_prompt_kernel,
        grid=(pairs, B, T // ATT_TILE),
        in_specs=in_specs,
        out_specs=pl.BlockSpec((1, ATT_TILE, LANES), lambda p, b, t: (b, t, p)),
        out_shape=jax.ShapeDtypeStruct((B, T, ATT_WIDTH), F32),
        scratch_shapes=[pltpu.VMEM((N_GROUPS, ATT_TILE, LANES), F32),
                        pltpu.VMEM((N_GROUPS, ATT_TILE, LANES), F32)],
        compiler_params=_params(("arbitrary", "arbitrary", "arbitrary")),
        name="attn_prompt",
    )(*args, bias)


def _sample_bias(g, S):
    win, dil = ATT_GROUPS[g]
    slopes = _alibi_slopes()[g]
    s_idx = np.tile(np.arange(S), HEADS)[:, None]
    slope = np.repeat(slopes, S)[:, None]

    def bias(pos, extra_valid):
        dist = win + s_idx - pos[None, :]
        valid = (dist >= 0) & (dist <= win) & (dist % dil == 0) & extra_valid[None, :]
        return np.where(valid, -slope * dist.astype(np.float32), np.float32(NEG_INF)).astype(np.float32)

    old = bias(np.arange(win), np.ones(win, bool))
    new = bias(win + np.arange(LANES) - (LANES - S), np.arange(LANES) >= LANES - S)
    return old, new


def _attn_sample_work(S, cache_ref, q_ref, k_ref, v_ref, bo_ref, bn_ref, mask_ref,
                      new_ref, o_ref, lse_ref):
    nq = cache_ref.shape[0]
    per_seq = [_attn_sample_seq_work(S, sq, cache_ref, q_ref, k_ref, v_ref, bo_ref, bn_ref, mask_ref,
                                     new_ref, o_ref, lse_ref) for sq in range(nq)]
    return [w[i] for i in range(len(per_seq[0])) for w in per_seq]


def _attn_sample_seq_work(S, sq, cache_ref, q_ref, k_ref, v_ref, bo_ref, bn_ref, mask_ref,
                          new_ref, o_ref, lse_ref):
    _, rows2, w = cache_ref.shape
    nch = w // LANES
    carry = {}

    def rotated(j):
        return pltpu.roll(cache_ref[sq, :, j * LANES:(j + 1) * LANES], LANES - S, 1)

    def new_tile():
        if "new" not in carry:
            kv_new = jnp.concatenate([k_ref[sq], v_ref[sq]], axis=1)
            carry["new"] = jnp.concatenate([jnp.zeros((LANES - S, rows2), F32), kv_new], axis=0).T
        return carry["new"]

    def shift_chunk(j):
        def run():
            keep = lax.broadcasted_iota(jnp.int32, (rows2, LANES), 1) < LANES - S
            cur = carry.pop("rot") if "rot" in carry else rotated(j)
            nxt = rotated(j + 1) if j + 1 < nch else new_tile()
            new_ref[sq, :, j * LANES:(j + 1) * LANES] = jnp.where(keep, cur, nxt)
            carry["rot"] = nxt
        return run

    def attention():
        mask = mask_ref[...]
        q = q_ref[sq] * (HEAD_DIM ** -0.5)
        qbd = (jnp.concatenate([q] * HEADS, axis=0) * mask).astype(BF16)
        kt_old = cache_ref[sq, 0:ATT_WIDTH, :].astype(BF16)
        vt_old = cache_ref[sq, ATT_WIDTH:rows2, :].astype(BF16)
        kt_new = new_tile()[0:ATT_WIDTH].astype(BF16)
        vt_new = new_tile()[ATT_WIDTH:rows2].astype(BF16)
        s_old = _dot(qbd, kt_old) + bo_ref[...]
        s_new = _dot(qbd, kt_new) + bn_ref[...]
        m = jnp.maximum(jnp.max(s_old, axis=-1, keepdims=True), jnp.max(s_new, axis=-1, keepdims=True))
        e_old = jnp.exp(s_old - m)
        e_new = jnp.exp(s_new - m)
        l = jnp.sum(e_old, axis=-1, keepdims=True) + jnp.sum(e_new, axis=-1, keepdims=True)
        o = (_dot_nt(e_old.astype(BF16), vt_old) + _dot_nt(e_new.astype(BF16), vt_new)) * (1.0 / l)
        o = o * mask
        lse = (m + jnp.log(l)) * mask
        o_acc, lse_acc = o[0:S], lse[0:S]
        for h in range(1, HEADS):
            o_acc = o_acc + o[h * S:(h + 1) * S]
            lse_acc = lse_acc + lse[h * S:(h + 1) * S]
        o_ref[sq] = o_acc
        lse_ref[sq] = lse_acc

    return [shift_chunk(j) for j in range(nch)] + [attention]


class _Part(NamedTuple):
    kernel: Callable
    in_specs: list
    args: list
    out_specs: list
    out_shapes: list
    scratch: list


def _run_parts(parts, grid, name, vmem_limit, interleave):
    n_in = [len(p.in_specs) for p in parts]
    n_out = [len(p.out_specs) for p in parts]
    n_scr = [len(p.scratch) for p in parts]

    def body(*refs):
        ins, outs, scr = refs[:sum(n_in)], refs[sum(n_in):sum(n_in) + sum(n_out)], refs[sum(n_in) + sum(n_out):]

        def refs_of(k):
            take = lambda seq, counts: seq[sum(counts[:k]):sum(counts[:k + 1])]
            return (*take(ins, n_in), *take(outs, n_out), *take(scr, n_scr))

        works = [parts[k].kernel(*refs_of(k)) for k in range(1, len(parts))]
        side = [w[i] for i in range(max(map(len, works), default=0)) for w in works if i < len(w)]
        parts[0].kernel(*refs_of(0), side=side, interleave=interleave)

    flat = lambda field: [x for p in parts for x in getattr(p, field)]
    res = pl.pallas_call(
        body,
        grid=grid,
        in_specs=flat("in_specs"),
        out_specs=flat("out_specs"),
        out_shape=flat("out_shapes"),
        scratch_shapes=flat("scratch"),
        compiler_params=_params(("arbitrary",) * len(grid), vmem_limit),
        name=name,
    )(*flat("args"))
    return [res[sum(n_out[:k]):sum(n_out[:k + 1])] for k in range(len(parts))]


def _attn_sample_part(Ps, cache, g, nq, batch_of):
    B, S, _ = Ps.shape
    w = cache.shape[1]
    assert B % nq == 0
    cache_t = jnp.transpose(cache, (0, 2, 3, 4, 1)).reshape(B, 2 * ATT_WIDTH, w)
    bo, bn = _sample_bias(g, S)
    mask = (np.arange(HEADS * S)[:, None] // S == np.arange(ATT_WIDTH)[None, :] // HEAD_DIM).astype(np.float32)
    col = lambda c: pl.BlockSpec((nq, S, ATT_WIDTH), lambda *ids: (batch_of(*ids), 0, c // ATT_WIDTH + g))
    const = lambda a: pl.BlockSpec(a.shape, lambda *ids: (0, 0))
    out_spec = pl.BlockSpec((nq, S, ATT_WIDTH), lambda *ids: (batch_of(*ids), 0, 0))
    win_spec = pl.BlockSpec((nq, 2 * ATT_WIDTH, w), lambda *ids: (batch_of(*ids), 0, 0))
    return _Part(
        kernel=functools.partial(_attn_sample_work, S),
        in_specs=[win_spec, col(COL_Q), col(COL_K), col(COL_V), const(bo), const(bn), const(mask)],
        args=[cache_t, Ps, Ps, Ps, jnp.asarray(bo), jnp.asarray(bn), jnp.asarray(mask)],
        out_specs=[win_spec, out_spec, out_spec],
        out_shapes=[jax.ShapeDtypeStruct((B, 2 * ATT_WIDTH, w), F32),
                    jax.ShapeDtypeStruct((B, S, ATT_WIDTH), F32),
                    jax.ShapeDtypeStruct((B, S, ATT_WIDTH), F32)],
        scratch=[])


def _window_from_position_minor(new, w):
    B = new.shape[0]
    return jnp.transpose(new.reshape(B, 2, HEADS, HEAD_DIM, w), (0, 4, 1, 2, 3))


def _kv_tail_kernel(k_ref, v_ref, out_ref):
    out_ref[0, 0:ATT_WIDTH, :] = k_ref[0].T
    out_ref[0, ATT_WIDTH:2 * ATT_WIDTH, :] = v_ref[0].T


def _kv_tail(P, g):
    B, T, _ = P.shape
    keep = min(ATT_GROUPS[g][0], T)
    tr = min(keep, 512)
    assert keep % tr == 0 and (T - keep) % tr == 0
    col = lambda c0: pl.BlockSpec((1, tr, ATT_WIDTH),
                                  lambda b, i: (b, (T - keep) // tr + i, c0 // ATT_WIDTH + g))
    out = pl.pallas_call(
        _kv_tail_kernel,
        grid=(B, keep // tr),
        in_specs=[col(COL_K), col(COL_V)],
        out_specs=pl.BlockSpec((1, 2 * ATT_WIDTH, tr), lambda b, i: (b, 0, i)),
        out_shape=jax.ShapeDtypeStruct((B, 2 * ATT_WIDTH, keep), F32),
        compiler_params=_params(("arbitrary", "arbitrary")),
        name=f"kv_tail_g{g}",
    )(P, P)
    return _window_from_position_minor(out, keep)[None]


def _ssd_kernel(L, *refs, side=(), interleave=False):
    n_xbc, n_z = CONV_DIM // SSD_IN_BLOCK, D_INNER // SSD_IN_BLOCK
    xbc_refs, z_refs = refs[:n_xbc], refs[n_xbc:n_xbc + n_z]
    (dt_ref, convin_ref, statein_ref, convw_ref, convb_ref, dtb_ref, alog_ref, dskip_ref, ng_ref, expand_ref,
     y_ref, state_ref, xpad_s, act_s, y_s, xw_s, fac_s) = refs[n_xbc + n_z:]
    side = list(side)
    n_stages = SSM_HEADS // 2
    per_stage = -(-len(side) // n_stages) if interleave else 0

    def run_side(n):
        for _ in range(min(n, len(side))):
            side.pop(0)()

    LP = SSD_CHUNK
    pad = CONV_WIDTH - 1
    base = 8
    hpg = SSM_HEADS // SSM_GROUPS
    gw = hpg * SSM_HEAD_DIM

    def sequence(q):
        @pl.when(pl.program_id(1) == 0)
        def _():
            state_ref[q] = statein_ref[q]
            xpad_s[q, 0:base, :] = jnp.zeros((base, CONV_DIM), F32)
            xpad_s[q, base - pad:base, :] = convin_ref[q]

        for i, xr in enumerate(xbc_refs):
            xpad_s[q, base:base + L, i * SSD_IN_BLOCK:(i + 1) * SSD_IN_BLOCK] = xr[q]
        cw = 512
        for cc in range(CONV_DIM // cw):
            cl = slice(cc * cw, (cc + 1) * cw)
            xfull = xpad_s[q, :, cl]
            x1 = pltpu.roll(xfull, 1, 0)
            u2 = pltpu.roll(convw_ref[1:2, cl] * xfull + convw_ref[0:1, cl] * x1, 2, 0)
            conv = convb_ref[:, cl] + convw_ref[3:4, cl] * xfull + convw_ref[2:3, cl] * x1 + u2
            act_s[q, :, cl] = _silu(conv[base:base + L])
        for i, xr in enumerate(xbc_refs):
            xpad_s[q, base - pad:base, i * SSD_IN_BLOCK:(i + 1) * SSD_IN_BLOCK] = xr[q, L - pad:L, :]
        yield

        dt = _softplus(dt_ref[q] + dtb_ref[...])
        row = lax.broadcasted_iota(jnp.int32, (LP, LP), 0)
        colm = lax.broadcasted_iota(jnp.int32, (LP, LP), 1)
        a = jnp.where(colm[0:1] < SSM_HEADS, -jnp.exp(alog_ref[...]), 0.0)
        da = _pad_rows(dt * a, LP)
        tril = (row >= colm).astype(F32)
        eye = (row == colm).astype(F32)
        cs = jnp.dot(tril, da, precision=HIGHEST, preferred_element_type=F32)
        cs2 = cs * LOG2E
        cs2_t = _dot_nt(eye, cs2, precision=HIGHEST)
        cs_l = cs2[0:L]
        cs_last = cs2[LP - 1:LP]
        ecs = jnp.exp2(cs_l)
        dte = jnp.exp2(cs_last - cs_l)
        etot = jnp.exp2(cs_last)
        causal = row[0:L] >= colm[0:L]
        lo = lax.broadcasted_iota(jnp.int32, (L, LANES), 1) < SSM_HEAD_DIM

        fac = jnp.concatenate([dt, ecs, dte], axis=0)
        f_hi = fac.astype(BF16)
        f_r = fac - f_hi.astype(F32)
        f_mid = f_r.astype(BF16)
        f_lo = (f_r - f_mid.astype(F32)).astype(BF16)
        pieces = _pad_rows(jnp.concatenate([f_hi, f_mid, f_lo], axis=1), fac_s.shape[1])
        fac_s[q] = _dot(pieces, expand_ref[...])
        yield

        for g in range(SSM_GROUPS):
            bg = act_s[q, :, D_INNER + g * D_STATE:D_INNER + (g + 1) * D_STATE]
            cg = act_s[q, :, D_INNER + SSM_GROUPS * D_STATE + g * D_STATE:
                       D_INNER + SSM_GROUPS * D_STATE + (g + 1) * D_STATE]
            bg_pad = _pad_rows(bg, LP).astype(BF16)
            cb = _dot_nt(cg.astype(BF16), bg_pad) if L >= 16 else _dot_nt(cg, _pad_rows(bg, LP))
            sg = state_ref[q, g * gw:(g + 1) * gw, :]
            yoff_g = _dot_nt(cg.astype(BF16), sg.astype(BF16)) if L >= 16 else _dot_nt(cg, sg)
            for hq in range(hpg // 2):
                ha = g * hpg + 2 * hq
                sl = slice(ha * SSM_HEAD_DIM, (ha + 2) * SSM_HEAD_DIM)
                xdt = act_s[q, :, sl] * fac_s[q, 0:L, sl]
                xdt_pad = _pad_rows(xdt, LP).astype(BF16)
                ys = []
                for h in (ha, ha + 1):
                    seg = cs_l[:, h:h + 1] - cs2_t[h:h + 1, :]
                    m = cb * jnp.exp2(jnp.where(causal, seg, -jnp.inf))
                    ys.append(_dot(m.astype(BF16), xdt_pad) if L >= 16 else _dot(m, xdt_pad.astype(F32)))
                ydiag = jnp.where(lo, ys[0], ys[1])
                yoff = yoff_g[:, 2 * hq * SSM_HEAD_DIM:(2 * hq + 2) * SSM_HEAD_DIM] * fac_s[q, L:2 * L, sl]
                y_s[q, :, sl] = ydiag + yoff
                xw_s[q, 0:L, 2 * hq * SSM_HEAD_DIM:(2 * hq + 2) * SSM_HEAD_DIM] = xdt * fac_s[q, 2 * L:3 * L, sl]
                yield
            if L < LP:
                xw_s[q, L:LP, :] = jnp.zeros((LP - L, gw), F32)
            upd = _dot(xw_s[q].T.astype(BF16), bg_pad)
            for hh in range(hpg):
                h = g * hpg + hh
                rows = slice(h * SSM_HEAD_DIM, (h + 1) * SSM_HEAD_DIM)
                urows = slice(hh * SSM_HEAD_DIM, (hh + 1) * SSM_HEAD_DIM)
                state_ref[q, rows, :] = state_ref[q, rows, :] * etot[0:1, h:h + 1] + upd[urows]

        for g in range(SSM_GROUPS):
            gl = slice(g * gw, (g + 1) * gw)
            yg = y_s[q, :, gl] + dskip_ref[:, gl] * act_s[q, :, gl]
            zb, zo = divmod(g * gw, SSD_IN_BLOCK)
            ug = yg * _silu(z_refs[zb][q, :, zo:zo + gw])
            ms = jnp.mean(ug * ug, axis=-1, keepdims=True)
            y_ref[q, :, gl] = (ug * lax.rsqrt(ms + NORM_EPS) * ng_ref[:, gl]).astype(y_ref.dtype)

    running = [sequence(q) for q in range(dt_ref.shape[0])]
    stage = 0
    while running:
        for gen in list(running):
            if next(gen, "done") == "done":
                running.remove(gen)
        if stage >= 2:
            run_side(per_stage)
        stage += 1
    run_side(len(side))


def _ssd_part(P, dt, conv_in, state_in, L, nq, y_dtype, conv_w, conv_b, dt_bias, a_log, d_skip, ssm_norm_g):
    B, T, _ = P.shape
    assert B % nq == 0
    pad_lanes = lambda v: jnp.pad(v.astype(F32), (0, LANES - SSM_HEADS)).reshape(1, LANES)
    dskip = jnp.repeat(d_skip.astype(F32), SSM_HEAD_DIM).reshape(1, D_INNER)
    const = lambda shape: pl.BlockSpec(shape, lambda b, c: (0, 0))
    spread = np.arange(LANES)[:, None] == np.arange(D_INNER)[None, :] // SSM_HEAD_DIM
    expand = jnp.asarray(np.tile(spread, (3, 1)), BF16)
    fac_rows = -(-3 * L // 16) * 16
    n_in_blocks = (CONV_DIM + D_INNER) // SSD_IN_BLOCK

    def col_block(col0, i):
        cb = col0 // SSD_IN_BLOCK + i
        return pl.BlockSpec((nq, L, SSD_IN_BLOCK), lambda b, c: (b, c, cb))

    return _Part(
        kernel=functools.partial(_ssd_kernel, L),
        in_specs=[col_block(COL_XBC, i) for i in range(CONV_DIM // SSD_IN_BLOCK)]
        + [col_block(COL_Z, i) for i in range(D_INNER // SSD_IN_BLOCK)]
        + [pl.BlockSpec((nq, L, LANES), lambda b, c: (b, c, 0)),
                  pl.BlockSpec((nq, CONV_WIDTH - 1, CONV_DIM), lambda b, c: (b, 0, 0)),
                  pl.BlockSpec((nq, D_INNER, D_STATE), lambda b, c: (b, 0, 0)),
                  const((CONV_WIDTH, CONV_DIM)), const((1, CONV_DIM)), const((1, LANES)),
                  const((1, LANES)), const((1, D_INNER)), const((1, D_INNER)),
                  pl.BlockSpec(expand.shape, lambda b, c: (0, 0), pipeline_mode=pl.Buffered(1))],
        out_specs=[pl.BlockSpec((nq, L, D_INNER), lambda b, c: (b, c, 0)),
                   pl.BlockSpec((nq, D_INNER, D_STATE), lambda b, c: (b, 0, 0))],
        out_shapes=[jax.ShapeDtypeStruct((B, T, D_INNER), y_dtype),
                    jax.ShapeDtypeStruct((B, D_INNER, D_STATE), F32)],
        scratch=[pltpu.VMEM((nq, 8 + L, CONV_DIM), F32),
                 pltpu.VMEM((nq, L, CONV_DIM), F32),
                 pltpu.VMEM((nq, L, D_INNER), F32),
                 pltpu.VMEM((nq, SSD_CHUNK, D_INNER // SSM_GROUPS), F32),
                 pltpu.VMEM((nq, fac_rows, D_INNER), F32)],
        args=[P] * n_in_blocks + [dt, conv_in, state_in.reshape(B, D_INNER, D_STATE), conv_w, conv_b.reshape(1, -1),
              pad_lanes(dt_bias), pad_lanes(a_log), dskip, ssm_norm_g.reshape(1, -1), expand])


def _back_kernel(n_att, *refs):
    att_refs = refs[:n_att]
    (gatt_ref, yssm_ref, ga_ref, gb_ref, x_ref, gate_ref, watt_ref, wssm_ref, wout_ref, fg_ref,
     y_ref) = refs[n_att:]
    bb, r, _ = x_ref.shape
    n = bb * r
    flat = lambda ref: ref[...].reshape(n, ref.shape[-1])
    if n_att == 1:
        att = flat(att_refs[0])
    else:
        os_, ls_ = att_refs[:N_GROUPS], att_refs[N_GROUPS:]
        l0, l1, l2 = flat(ls_[0]), flat(ls_[1]), flat(ls_[2])
        m = jnp.maximum(jnp.maximum(l0, l1), l2)
        e0, e1, e2 = jnp.exp(l0 - m), jnp.exp(l1 - m), jnp.exp(l2 - m)
        att = (e0 * flat(os_[0]) + e1 * flat(os_[1]) + e2 * flat(os_[2])) * (1.0 / (e0 + e1 + e2))
    a_out = _dot((att * _silu(flat(gatt_ref))).astype(BF16), watt_ref[...])
    m_out = _dot(flat(yssm_ref).astype(BF16), wssm_ref[...])
    merged = _sigmoid(flat(ga_ref)) * a_out + _sigmoid(flat(gb_ref)) * m_out
    res = _dot(merged.astype(BF16), wout_ref[...]).reshape(bb, r, D_MODEL)
    xo = x_ref[...] + gate_ref[...] * res
    ms = jnp.mean(xo * xo, axis=-1, keepdims=True)
    y_ref[...] = xo * lax.rsqrt(ms + NORM_EPS) * fg_ref[...]


def _back(x, mod, P, atts, y_ssm, w_att, w_ssm, w_out, final_g, bb, r):
    nb, rr, _ = x.shape
    grid = (nb // bb, rr // r)
    row = lambda width, col: pl.BlockSpec((bb, r, width), lambda b, i: (b, i, col // width))
    const = lambda a: pl.BlockSpec(a.shape, lambda b, i: (0,) * a.ndim)
    fg = final_g.reshape(1, -1)
    return pl.pallas_call(
        functools.partial(_back_kernel, len(atts)),
        grid=grid,
        in_specs=[row(ATT_WIDTH, 0)] * len(atts)
        + [row(ATT_WIDTH, COL_GATT), row(D_INNER, 0), row(D_MODEL, COL_GA), row(D_MODEL, COL_GB),
           row(D_MODEL, 0), pl.BlockSpec((bb, 1, D_MODEL), lambda b, i: (b, 0, 2)),
           const(w_att), const(w_ssm), const(w_out), const(fg)],
        out_specs=row(D_MODEL, 0),
        out_shape=jax.ShapeDtypeStruct(x.shape, F32),
        compiler_params=_params(("arbitrary", "arbitrary")),
        name="back_proj",
    )(*atts, P, y_ssm, P, P, x, mod, w_att, w_ssm, w_out, fg)


def kernel(x_prompt, x_sample, c_prompt, c_sample, cache_kv_w128, cache_kv_w512, cache_kv_w2048,
           state_ssm, state_conv, norm_g, w_ada, b_ada, w_in, conv_w, conv_b, dt_bias, a_log,
           d_skip, ssm_norm_g, w_att_branch, w_ssm_branch, w_out, final_norm_g):
    depth = w_in.shape[0]
    assert depth == 1
    B, T, _ = x_prompt.shape
    Bs, S, _ = x_sample.shape
    caches = (cache_kv_w128, cache_kv_w512, cache_kv_w2048)
    l = 0

    offs = np.cumsum((0,) + IN_SIZES)
    w_a = w_in[l].astype(BF16)
    w_b = w_a[:, offs[7]:]
    w_dt = jnp.pad(w_a[:, offs[6]:offs[7]], ((0, 0), (0, LANES - SSM_HEADS)))
    assert offs[6] == COL_GA and w_b.shape[1] == P_WIDTH - COL_GA
    w_att = w_att_branch[l].astype(BF16)
    w_ssm = w_ssm_branch[l].astype(BF16)
    w_o = w_out[l].astype(BF16)

    n_pad = -(B + Bs) % 8
    c_all = jnp.concatenate([c_prompt, c_sample, jnp.zeros((n_pad, D_MODEL), F32)], axis=0)
    mod = _modulation(c_all, w_ada[l], b_ada[l])
    mod_p = mod[:B].reshape(B, 1, 3 * D_MODEL)
    mod_s = mod[B:B + Bs].reshape(Bs, 1, 3 * D_MODEL)

    ssm_args = (conv_w[l], conv_b[l], dt_bias[l], a_log[l], d_skip[l], ssm_norm_g[l])

    Pp, dtp = _front(x_prompt, mod_p, norm_g[l], w_a, w_b, w_dt, bb=1, r=1024, tn=1024)
    Ps, dts = _front(x_sample, mod_s, norm_g[l], w_a, w_b, w_dt, bb=Bs, r=S, tn=1024)

    zeros_state = jnp.zeros((B, SSM_HEADS, SSM_HEAD_DIM, D_STATE), F32)
    zeros_conv = jnp.zeros((B, CONV_WIDTH - 1, CONV_DIM), F32)
    nc = T // SSD_CHUNK
    assert Bs == B * nc
    (y_ssm_p, ssm_p), (new2, o2, lse2) = _run_parts(
        [_ssd_part(Pp, dtp, zeros_conv, zeros_state, SSD_CHUNK, 1, BF16, *ssm_args),
         _attn_sample_part(Ps, caches[2][l], 2, 1, lambda b, c: b * nc + c)],
        grid=(B, nc), name="ssd_prompt_window2", vmem_limit=FUSED_VMEM_LIMIT, interleave=True)
    nq = SAMPLE_SEQS_PER_STEP
    (y_ssm_s, ssm_s), (new0, o0, lse0), (new1, o1, lse1) = _run_parts(
        [_ssd_part(Ps, dts, state_conv[l], state_ssm[l], S, nq, F32, *ssm_args),
         _attn_sample_part(Ps, caches[0][l], 0, nq, lambda b, c: b),
         _attn_sample_part(Ps, caches[1][l], 1, nq, lambda b, c: b)],
        grid=(Bs // nq, 1), name="ssd_sample_window01", vmem_limit=VMEM_LIMIT, interleave=True)
    ssm_p = ssm_p.reshape(zeros_state.shape)
    ssm_s = ssm_s.reshape(state_ssm[l].shape)
    kv_s = [_window_from_position_minor(new, new.shape[-1])[None] for new in (new0, new1, new2)]

    att_p = _attn_prompt(Pp)
    kv_p = [_kv_tail(Pp, g) for g in range(N_GROUPS)]
    conv_p = Pp[:, T - (CONV_WIDTH - 1):, COL_XBC:COL_XBC + CONV_DIM]
    y_prompt = _back(x_prompt, mod_p, Pp, [att_p], y_ssm_p, w_att, w_ssm, w_o, final_norm_g, bb=1, r=512)

    conv_s = jnp.concatenate([state_conv[l], Ps[:, :, COL_XBC:COL_XBC + CONV_DIM]], axis=1)[:, -(CONV_WIDTH - 1):]
    y_sample = _back(x_sample, mod_s, Ps, [o0, o1, o2, lse0, lse1, lse2], y_ssm_s, w_att, w_ssm, w_o,
                     final_norm_g, bb=32, r=S)

    return (y_prompt, y_sample, kv_p[0], kv_p[1], kv_p[2], ssm_p[None], conv_p[None],
            kv_s[0], kv_s[1], kv_s[2], ssm_s[None], conv_s[None])
```
